```python
import math
import jax, jax.numpy as jnp
from jax import lax
import numpy as np

D_MODEL = 1024
BATCH = 4
SEQ = 4096
DEPTH = 4

D_MIX = D_MODEL
D_POOL = D_MIX // 2
D_SSM = D_MIX - D_POOL
POOL_WINDOWS = (2, 4, 8, 16)
N_POOL_GROUPS = len(POOL_WINDOWS)
POOL_GROUP = D_POOL // N_POOL_GROUPS
SSM_GROUP = 16
N_SSM_GROUPS = D_SSM // SSM_GROUP
SSM_STATE = 64
D_FF = -(-8 * D_MODEL // (3 * 256)) * 256
RMS_EPS = 1e-6
DT_MIN = 1e-3
DT_MAX = 1e-1

kernel_name = "hybrid_pool_s5_parallel_heads"


def rmsnorm(x, g):
    xf = x.astype(jnp.float32)
    y = xf * lax.rsqrt(jnp.mean(xf * xf, axis=-1, keepdims=True) + RMS_EPS)
    return y.astype(x.dtype) * g


def pool_mixer(u, w_pool, scale):
    b, l, _ = u.shape
    ug = u.astype(jnp.float32).reshape(b, l, N_POOL_GROUPS, POOL_GROUP)
    cs = jnp.cumsum(ug, axis=1)
    n_pos = jnp.arange(1, l + 1, dtype=jnp.float32)
    diffs = []
    for gi, w in enumerate(POOL_WINDOWS):
        c = cs[:, :, gi]
        lagged = jnp.pad(c, ((0, 0), (w, 0), (0, 0)))[:, :l]
        mean = (c - lagged) / jnp.minimum(n_pos, float(w))[None, :, None]
        diffs.append(mean - ug[:, :, gi])
    d = jnp.stack(diffs, axis=2)
    y = jnp.einsum('blgc,gcd->blgd', d, w_pool.astype(jnp.float32)).reshape(b, l, D_POOL)
    return (y * scale.astype(jnp.float32)).astype(u.dtype)


def _complex_affine_combine(e1, e2):
    a1r, a1i, b1r, b1i = e1
    a2r, a2i, b2r, b2i = e2
    ar = a2r * a1r - a2i * a1i
    ai = a2r * a1i + a2i * a1r
    br = a2r * b1r - a2i * b1i + b2r
    bi = a2r * b1i + a2i * b1r + b2i
    return ar, ai, br, bi


def ssm_mixer(u, lam_re, lam_im, log_dt, b_re, b_im, c_re, c_im, d_skip, w_glu, b_glu):
    f32 = jnp.float32
    bsz, l, _ = u.shape
    ug = u.astype(f32).reshape(bsz, l, N_SSM_GROUPS, SSM_GROUP)
    lr, li = lam_re.astype(f32), lam_im.astype(f32)
    dt = jnp.exp(log_dt.astype(f32))[:, None]
    mag = jnp.exp(lr * dt)
    abar_r = mag * jnp.cos(li * dt)
    abar_i = mag * jnp.sin(li * dt)
    den = lr * lr + li * li
    nr, ni = abar_r - 1.0, abar_i
    coef_r = (nr * lr + ni * li) / den
    coef_i = (ni * lr - nr * li) / den
    br, bi = b_re.astype(f32), b_im.astype(f32)
    bbar_r = coef_r[..., None] * br - coef_i[..., None] * bi
    bbar_i = coef_r[..., None] * bi + coef_i[..., None] * br
    bu_r = jnp.einsum('blgh,gph->blgp', ug, bbar_r)
    bu_i = jnp.einsum('blgh,gph->blgp', ug, bbar_i)
    a_r = jnp.broadcast_to(abar_r, (1, l, N_SSM_GROUPS, SSM_STATE))
    a_i = jnp.broadcast_to(abar_i, (1, l, N_SSM_GROUPS, SSM_STATE))
    _, _, s_r, s_i = lax.associative_scan(_complex_affine_combine, (a_r, a_i, bu_r, bu_i), axis=1)
    y = (jnp.einsum('blgp,ghp->blgh', s_r, c_re.astype(f32))
         - jnp.einsum('blgp,ghp->blgh', s_i, c_im.astype(f32))
         + d_skip.astype(f32) * ug).reshape(bsz, l, D_SSM)
    y = jax.nn.gelu(y)
    y = y * jax.nn.sigmoid(y @ w_glu.astype(f32) + b_glu.astype(f32))
    return y.astype(u.dtype)


def swiglu(h, w_gate, w_up, w_down):
    return (jax.nn.silu(h @ w_gate) * (h @ w_up)) @ w_down


def setup_inputs(seed: int = 0) -> dict:
    key = jax.random.key(seed)
    ks = jax.random.split(key, 24)
    f32 = jnp.float32
    nrm = lambda k, s, sc: jax.random.normal(k, s, f32) * sc
    res_scale = (2 * DEPTH) ** -0.5
    n_idx = jnp.arange(SSM_STATE, dtype=f32)
    lam_re = -0.5 + nrm(ks[5], (DEPTH, N_SSM_GROUPS, SSM_STATE), 0.01)
    lam_im = math.pi * n_idx[None, None, :] + nrm(ks[6], (DEPTH, N_SSM_GROUPS, SSM_STATE), 0.01)
    log_dt = jax.random.uniform(ks[7], (DEPTH, N_SSM_GROUPS), f32, math.log(DT_MIN), math.log(DT_MAX))
    return {
        "x": nrm(ks[0], (BATCH, SEQ, D_MODEL), 1.0),
        "norm_mix": 1.0 + nrm(ks[1], (DEPTH, D_MODEL), 0.02),
        "w_in": nrm(ks[2], (DEPTH, D_MODEL, D_MIX), D_MODEL ** -0.5),
        "w_pool": nrm(ks[3], (DEPTH, N_POOL_GROUPS, POOL_GROUP, POOL_GROUP), POOL_GROUP ** -0.5),
        "pool_scale": 1.0 + nrm(ks[4], (DEPTH, D_POOL), 0.02),
        "lam_re": lam_re,
        "lam_im": lam_im,
        "log_dt": log_dt,
        "b_re": nrm(ks[8], (DEPTH, N_SSM_GROUPS, SSM_STATE, SSM_GROUP), (2 * SSM_GROUP) ** -0.5),
        "b_im": nrm(ks[9], (DEPTH, N_SSM_GROUPS, SSM_STATE, SSM_GROUP), (2 * SSM_GROUP) ** -0.5),
        "c_re": nrm(ks[10], (DEPTH, N_SSM_GROUPS, SSM_GROUP, SSM_STATE), (2 * SSM_STATE) ** -0.5),
        "c_im": nrm(ks[11], (DEPTH, N_SSM_GROUPS, SSM_GROUP, SSM_STATE), (2 * SSM_STATE) ** -0.5),
        "d_skip": nrm(ks[12], (DEPTH, N_SSM_GROUPS, SSM_GROUP), 1.0),
        "w_glu": nrm(ks[13], (DEPTH, D_SSM, D_SSM), D_SSM ** -0.5),
        "b_glu": nrm(ks[14], (DEPTH, D_SSM), 0.01),
        "w_out": nrm(ks[15], (DEPTH, D_MIX, D_MODEL), D_MIX ** -0.5 * res_scale),
        "norm_ffn": 1.0 + nrm(ks[16], (DEPTH, D_MODEL), 0.02),
        "w_gate": nrm(ks[17], (DEPTH, D_MODEL, D_FF), D_MODEL ** -0.5),
        "w_up": nrm(ks[18], (DEPTH, D_MODEL, D_FF), D_MODEL ** -0.5),
        "w_down": nrm(ks[19], (DEPTH, D_FF, D_MODEL), D_FF ** -0.5 * res_scale),
        "norm_final": 1.0 + nrm(ks[20], (D_MODEL,), 0.02),
    }


def reference(x, norm_mix, w_in, w_pool, pool_scale, lam_re, lam_im, log_dt, b_re, b_im,
              c_re, c_im, d_skip, w_glu, b_glu, w_out, norm_ffn, w_gate, w_up, w_down,
              norm_final):
    h = x
    for i in range(DEPTH):
        u = rmsnorm(h, norm_mix[i]) @ w_in[i]
        u_pool, u_ssm = u[..., :D_POOL], u[..., D_POOL:]
        y_pool = pool_mixer(u_pool, w_pool[i], pool_scale[i])
        y_ssm = ssm_mixer(u_ssm, lam_re[i], lam_im[i], log_dt[i], b_re[i], b_im[i],
                          c_re[i], c_im[i], d_skip[i], w_glu[i], b_glu[i])
        h = h + jnp.concatenate([y_pool, y_ssm], axis=-1) @ w_out[i]
        h = h + swiglu(rmsnorm(h, norm_ffn[i]), w_gate[i], w_up[i], w_down[i])
    return rmsnorm(h, norm_final)
```

```python
import functools
import math

import jax
import jax.numpy as jnp
from jax import lax
from jax.experimental import pallas as pl
from jax.experimental.pallas import tpu as pltpu

F32 = jnp.float32
BF16 = jnp.bfloat16

D_MODEL = 1024
BATCH = 4
SEQ = 4096
DEPTH = 4
D_POOL = 512
D_SSM = 512
POOL_WINDOWS = (2, 4, 8, 16)
POOL_GROUP = 128
SSM_GROUP = 16
N_SSM_GROUPS = 32
SSM_STATE = 64
D_FF = 2816
RMS_EPS = 1e-6

V7X_SUBLANES = 8
V7X_LANES = 128
V7X_MXU_DIM = 256
V7X_VMEM_LIMIT_BYTES = 56 * 1024 * 1024

SSM_HALVES = D_SSM // V7X_MXU_DIM
GROUPS_PER_HALF = N_SSM_GROUPS // SSM_HALVES
STATE_PER_HALF = GROUPS_PER_HALF * SSM_STATE
RE_IM = 2

TIME_BLOCK = 128
MIX_ROWS = TIME_BLOCK * BATCH
POOL_HIST_STEPS = max(POOL_WINDOWS)
POOL_HIST_ROWS = POOL_HIST_STEPS * BATCH
FFN_ROWS = 512

assert BATCH * RE_IM == V7X_SUBLANES, "the scan tile layout needs batch*2 == 8 sublanes"


def _rmsnorm(x, g):
    return x * lax.rsqrt(jnp.mean(x * x, axis=-1, keepdims=True) + RMS_EPS) * g


def _sigmoid(x):
    return 1.0 / (1.0 + jnp.exp(-x))


def _ssm_prep_kernel(lr_ref, li_ref, ldt_ref, br_ref, bi_ref, ar_ref, ai_ref, bbr_ref, bbi_ref):
    lr = lr_ref[...]
    li = li_ref[...]
    dt = jnp.exp(ldt_ref[...])
    mag = jnp.exp(lr * dt)
    abar_r = mag * jnp.cos(li * dt)
    abar_i = mag * jnp.sin(li * dt)
    den = lr * lr + li * li
    nr = abar_r - 1.0
    ni = abar_i
    coef_r = (nr * lr + ni * li) / den
    coef_i = (ni * lr - nr * li) / den
    br = br_ref[...]
    bi = bi_ref[...]
    ar_ref[...] = abar_r
    ai_ref[...] = abar_i
    bbr_ref[...] = coef_r * br - coef_i * bi
    bbi_ref[...] = coef_r * bi + coef_i * br


def _ssm_prep(lam_re, lam_im, log_dt, b_re, b_im):
    rows = DEPTH * N_SSM_GROUPS * SSM_GROUP

    def per_channel(a):
        return jnp.repeat(a.reshape(DEPTH * N_SSM_GROUPS, SSM_STATE), SSM_GROUP, axis=0)

    lr = per_channel(lam_re)
    li = per_channel(lam_im)
    ldt = per_channel(jnp.broadcast_to(log_dt[..., None], lam_re.shape))
    br = jnp.transpose(b_re, (0, 1, 3, 2)).reshape(rows, SSM_STATE)
    bi = jnp.transpose(b_im, (0, 1, 3, 2)).reshape(rows, SSM_STATE)
    out = jax.ShapeDtypeStruct((rows, SSM_STATE), F32)
    return pl.pallas_call(
        _ssm_prep_kernel,
        out_shape=(out, out, out, out),
        name="ssm_prep",
    )(lr, li, ldt, br, bi)


def _mixer_kernel(h_ref, gmix_ref, win_ref, wpool_ref, pscale_ref, wb_ref, ar_ref, ai_ref,
                  wc_ref, dskip_ref, wglu_ref, bglu_ref, wout_ref, o_ref,
                  hist_ref, state_ref, pbuf_ref, bu_ref):
    blk = pl.program_id(0)

    @pl.when(blk == 0)
    def _():
        hist_ref[...] = jnp.zeros_like(hist_ref)
        state_ref[...] = jnp.zeros_like(state_ref)

    h = h_ref[...]
    hn = _rmsnorm(h, gmix_ref[...])
    u = jnp.dot(hn.astype(BF16), win_ref[...], preferred_element_type=F32)

    up = u[:, :D_POOL]
    pbuf_ref[0:POOL_HIST_ROWS, :] = hist_ref[...]
    pbuf_ref[POOL_HIST_ROWS:, :] = up
    hist_ref[...] = up[MIX_ROWS - POOL_HIST_ROWS:, :]
    ext_rows = MIX_ROWS + POOL_HIST_ROWS
    row = lax.broadcasted_iota(jnp.int32, (MIX_ROWS, POOL_GROUP), 0)
    n_pos = blk * TIME_BLOCK + row // BATCH + 1
    y_pool = []
    for gi, w in enumerate(POOL_WINDOWS):
        x = pbuf_ref[:, gi * POOL_GROUP:(gi + 1) * POOL_GROUP]
        s = x
        span = 1
        while span < w:
            s = s + pltpu.roll(s, span * BATCH, 0)
            span *= 2
        mean = s[POOL_HIST_ROWS:, :] / jnp.minimum(n_pos, w).astype(F32)
        d = mean - x[POOL_HIST_ROWS:, :]
        y_pool.append(jnp.dot(d.astype(BF16), wpool_ref[gi], preferred_element_type=F32))
    y_pool = jnp.concatenate(y_pool, axis=1) * pscale_ref[...]
    del ext_rows

    us = u[:, D_POOL:]
    usb = us.astype(BF16)
    lo = lax.broadcasted_iota(jnp.int32, (V7X_SUBLANES, STATE_PER_HALF), 0) < BATCH
    y_halves = []
    for j in range(SSM_HALVES):
        bu_ref[j] = jnp.dot(usb[:, j * V7X_MXU_DIM:(j + 1) * V7X_MXU_DIM], wb_ref[j],
                            preferred_element_type=F32)
        a_r = ar_ref[j]
        a_i = ai_ref[j]

        def body(m, v, j=j, a_r=a_r, a_i=a_i):
            r0 = pl.multiple_of(m * V7X_SUBLANES, V7X_SUBLANES)
            re = bu_ref[j, pl.ds(r0, V7X_SUBLANES), 0:STATE_PER_HALF]
            im = bu_ref[j, pl.ds(r0, V7X_SUBLANES), STATE_PER_HALF:]
            x0 = jnp.where(lo, re, pltpu.roll(im, BATCH, 0))
            x1 = jnp.where(lo, pltpu.roll(re, BATCH, 0), im)
            v0 = a_r * v + a_i * pltpu.roll(v, BATCH, 0) + x0
            v1 = a_r * v0 + a_i * pltpu.roll(v0, BATCH, 0) + x1
            bu_ref[j, pl.ds(r0, V7X_SUBLANES), 0:STATE_PER_HALF] = jnp.where(
                lo, v0, pltpu.roll(v1, BATCH, 0))
            bu_ref[j, pl.ds(r0, V7X_SUBLANES), STATE_PER_HALF:] = jnp.where(
                lo, pltpu.roll(v0, BATCH, 0), v1)
            return v1

        state_ref[j] = lax.fori_loop(0, MIX_ROWS // V7X_SUBLANES, body, state_ref[j], unroll=2)
        y_halves.append(jnp.dot(bu_ref[j].astype(BF16), wc_ref[j], preferred_element_type=F32))
    y = jnp.concatenate(y_halves, axis=1) + dskip_ref[...] * us
    y = 0.5 * y * (1.0 + jnp.tanh(math.sqrt(2.0 / math.pi) * (y + 0.044715 * (y * y * y))))
    z = jnp.dot(y.astype(BF16), wglu_ref[...], preferred_element_type=F32) + bglu_ref[...]
    y_ssm = y * _sigmoid(z)

    cat = jnp.concatenate([y_pool, y_ssm], axis=1).astype(BF16)
    o_ref[...] = h + jnp.dot(cat, wout_ref[...], preferred_element_type=F32)


def _const_spec(shape):
    zeros = (0,) * len(shape)
    return pl.BlockSpec(shape, lambda i, z=zeros: z)


def _mixer_layer(h, gmix, win, wpool, pscale, wb, a_r, a_i, wc, dskip, wglu, bglu, wout):
    n_tok = h.shape[0]
    row_spec = pl.BlockSpec((MIX_ROWS, D_MODEL), lambda i: (i, 0))
    consts = (gmix, win, wpool, pscale, wb, a_r, a_i, wc, dskip, wglu, bglu, wout)
    return pl.pallas_call(
        _mixer_kernel,
        grid=(n_tok // MIX_ROWS,),
        in_specs=[row_spec] + [_const_spec(c.shape) for c in consts],
        out_specs=row_spec,
        out_shape=jax.ShapeDtypeStruct((n_tok, D_MODEL), F32),
        scratch_shapes=[
            pltpu.VMEM((POOL_HIST_ROWS, D_POOL), F32),
            pltpu.VMEM((SSM_HALVES, V7X_SUBLANES, STATE_PER_HALF), F32),
            pltpu.VMEM((MIX_ROWS + POOL_HIST_ROWS, D_POOL), F32),
            pltpu.VMEM((SSM_HALVES, MIX_ROWS, RE_IM * STATE_PER_HALF), F32),
        ],
        compiler_params=pltpu.CompilerParams(
            dimension_semantics=("arbitrary",), vmem_limit_bytes=V7X_VMEM_LIMIT_BYTES),
        name="mixer",
    )(h, *consts)


def _ffn_kernel(h_ref, gffn_ref, wg_ref, wu_ref, wd_ref, gfin_ref, o_ref, *, final):
    h = h_ref[...]
    hn = _rmsnorm(h, gffn_ref[...]).astype(BF16)
    g = jnp.dot(hn, wg_ref[...], preferred_element_type=F32)
    u = jnp.dot(hn, wu_ref[...], preferred_element_type=F32)
    a = (g * _sigmoid(g) * u).astype(BF16)
    out = h + jnp.dot(a, wd_ref[...], preferred_element_type=F32)
    if final:
        out = _rmsnorm(out, gfin_ref[...])
    o_ref[...] = out


def _ffn_layer(h, gffn, wg, wu, wd, gfin, final):
    n_tok = h.shape[0]
    row_spec = pl.BlockSpec((FFN_ROWS, D_MODEL), lambda i: (i, 0))

    def resident(shape):
        return pl.BlockSpec(shape, lambda i: (0, 0), pipeline_mode=pl.Buffered(1))

    return pl.pallas_call(
        functools.partial(_ffn_kernel, final=final),
        grid=(n_tok // FFN_ROWS,),
        in_specs=[row_spec, resident(gffn.shape), resident(wg.shape), resident(wu.shape),
                  resident(wd.shape), resident(gfin.shape)],
        out_specs=row_spec,
        out_shape=jax.ShapeDtypeStruct((n_tok, D_MODEL), F32),
        compiler_params=pltpu.CompilerParams(
            dimension_semantics=("parallel",), vmem_limit_bytes=V7X_VMEM_LIMIT_BYTES),
        name="ffn_final" if final else "ffn",
    )(h, gffn, wg, wu, wd, gfin)


def _block_diag_in(bb):
    eye = jnp.eye(GROUPS_PER_HALF, dtype=F32)
    bb = bb.reshape(DEPTH, SSM_HALVES, GROUPS_PER_HALF, SSM_GROUP, SSM_STATE)
    return jnp.einsum('djghp,gk->djghkp', bb, eye).reshape(
        DEPTH, SSM_HALVES, GROUPS_PER_HALF * SSM_GROUP, STATE_PER_HALF)


def _block_diag_out(cc):
    eye = jnp.eye(GROUPS_PER_HALF, dtype=F32)
    cc = cc.reshape(DEPTH, SSM_HALVES, GROUPS_PER_HALF, SSM_GROUP, SSM_STATE)
    return jnp.einsum('djghp,gk->djgpkh', cc, eye).reshape(
        DEPTH, SSM_HALVES, STATE_PER_HALF, GROUPS_PER_HALF * SSM_GROUP)


def kernel(x, norm_mix, w_in, w_pool, pool_scale, lam_re, lam_im, log_dt, b_re, b_im, c_re, c_im,
           d_skip, w_glu, b_glu, w_out, norm_ffn, w_gate, w_up, w_down, norm_final):
    assert x.shape == (BATCH, SEQ, D_MODEL) and x.dtype == F32

    abar_r, abar_i, bbar_r, bbar_i = _ssm_prep(lam_re, lam_im, log_dt, b_re, b_im)
    per_group = (DEPTH, N_SSM_GROUPS, SSM_GROUP, SSM_STATE)
    bbar_r = bbar_r.reshape(per_group)
    bbar_i = bbar_i.reshape(per_group)
    abar_r = abar_r.reshape(per_group)[:, :, 0, :].reshape(DEPTH, SSM_HALVES, 1, STATE_PER_HALF)
    abar_i = abar_i.reshape(per_group)[:, :, 0, :].reshape(DEPTH, SSM_HALVES, 1, STATE_PER_HALF)
    a_r = jnp.broadcast_to(abar_r, (DEPTH, SSM_HALVES, V7X_SUBLANES, STATE_PER_HALF))
    sign = jnp.where(jnp.arange(V7X_SUBLANES) < BATCH, -1.0, 1.0).astype(F32)[None, None, :, None]
    a_i = sign * abar_i
    wb = jnp.concatenate([_block_diag_in(bbar_r), _block_diag_in(bbar_i)], axis=-1).astype(BF16)
    wc = jnp.concatenate([_block_diag_out(c_re), -_block_diag_out(c_im)], axis=2).astype(BF16)

    h = jnp.transpose(x, (1, 0, 2)).reshape(SEQ * BATCH, D_MODEL)
    gfin = norm_final.reshape(1, D_MODEL)
    for i in range(DEPTH):
        h = _mixer_layer(
            h, norm_mix[i].reshape(1, D_MODEL), w_in[i].astype(BF16), w_pool[i].astype(BF16),
            pool_scale[i].reshape(1, D_POOL), wb[i], a_r[i], a_i[i], wc[i],
            d_skip[i].reshape(1, D_SSM), w_glu[i].astype(BF16), b_glu[i].reshape(1, D_SSM),
            w_out[i].astype(BF16))
        h = _ffn_layer(h, norm_ffn[i].reshape(1, D_MODEL), w_gate[i].astype(BF16),
                       w_up[i].astype(BF16), w_down[i].astype(BF16), gfin, final=(i == DEPTH - 1))
    return jnp.transpose(h.reshape(SEQ, BATCH, D_MODEL), (1, 0, 2))
```

```python
import functools
import math

import jax
import jax.numpy as jnp
from jax import lax
from jax.experimental import pallas as pl
from jax.experimental.pallas import tpu as pltpu

F32 = jnp.float32
BF16 = jnp.bfloat16

D_MODEL = 1024
BATCH = 4
SEQ = 4096
DEPTH = 4
D_POOL = 512
D_SSM = 512
POOL_WINDOWS = (2, 4, 8, 16)
POOL_GROUP = 128
SSM_GROUP = 16
N_SSM_GROUPS = 32
SSM_STATE = 64
D_FF = 2816
RMS_EPS = 1e-6

V7X_SUBLANES = 8
V7X_LANES = 128
V7X_MXU_DIM = 256
V7X_VMEM_LIMIT_BYTES = 56 * 1024 * 1024

SSM_HALVES = D_SSM // V7X_MXU_DIM
GROUPS_PER_HALF = N_SSM_GROUPS // SSM_HALVES
CH_PER_HALF = GROUPS_PER_HALF * SSM_GROUP
STATE_PER_HALF = GROUPS_PER_HALF * SSM_STATE
RE_IM = 2

TIME_BLOCK = 128
BLOCK_ROWS = TIME_BLOCK * BATCH
POOL_HIST_STEPS = max(POOL_WINDOWS)
POOL_HIST_ROWS = POOL_HIST_STEPS * BATCH
N_BLOCKS = SEQ // TIME_BLOCK
LANE_SLABS = D_MODEL // V7X_LANES

assert BATCH * RE_IM == V7X_SUBLANES, "the scan tile layout needs batch*2 == 8 sublanes"


def _rmsnorm(x, g):
    return x * lax.rsqrt(jnp.mean(x * x, axis=-1, keepdims=True) + RMS_EPS) * g


def _sigmoid(x):
    return 1.0 / (1.0 + jnp.exp(-x))


def _ssm_prep_kernel(lr_ref, li_ref, ldt_ref, br_ref, bi_ref, ar_ref, ai_ref, bbr_ref, bbi_ref):
    lr = lr_ref[...]
    li = li_ref[...]
    dt = jnp.exp(ldt_ref[...])
    mag = jnp.exp(lr * dt)
    abar_r = mag * jnp.cos(li * dt)
    abar_i = mag * jnp.sin(li * dt)
    den = lr * lr + li * li
    nr = abar_r - 1.0
    ni = abar_i
    coef_r = (nr * lr + ni * li) / den
    coef_i = (ni * lr - nr * li) / den
    br = br_ref[...]
    bi = bi_ref[...]
    ar_ref[...] = abar_r
    ai_ref[...] = abar_i
    bbr_ref[...] = coef_r * br - coef_i * bi
    bbi_ref[...] = coef_r * bi + coef_i * br


def _ssm_prep(lam_re, lam_im, log_dt, b_re, b_im):
    rows = DEPTH * N_SSM_GROUPS * SSM_GROUP

    def per_channel(a):
        a = jnp.broadcast_to(a[:, :, None, :], (DEPTH, N_SSM_GROUPS, SSM_GROUP, SSM_STATE))
        return a.reshape(rows, SSM_STATE)

    lr = per_channel(lam_re)
    li = per_channel(lam_im)
    ldt = per_channel(jnp.broadcast_to(log_dt[..., None], lam_re.shape))
    br = jnp.transpose(b_re, (0, 1, 3, 2)).reshape(rows, SSM_STATE)
    bi = jnp.transpose(b_im, (0, 1, 3, 2)).reshape(rows, SSM_STATE)
    out = jax.ShapeDtypeStruct((rows, SSM_STATE), F32)
    return pl.pallas_call(
        _ssm_prep_kernel,
        out_shape=(out, out, out, out),
        name="ssm_prep",
    )(lr, li, ldt, br, bi)


def _mixer_kernel(h_ref, gmix_ref, win_ref, wpool_ref, pscale_ref, wb_ref, ar_ref, ai_ref,
                  wc_ref, dskip_ref, wglu_ref, bglu_ref, wout_ref, o_ref,
                  hist_ref, state_ref, pbuf_ref, bu_ref, hbuf_ref, *, first):
    blk = pl.program_id(0)

    @pl.when(blk == 0)
    def _():
        hist_ref[...] = jnp.zeros_like(hist_ref)
        state_ref[...] = jnp.zeros_like(state_ref)

    if first:
        for k in range(LANE_SLABS):
            for b in range(BATCH):
                hbuf_ref[k, pl.ds(b, TIME_BLOCK, stride=BATCH), :] = (
                    h_ref[b, :, k * V7X_LANES:(k + 1) * V7X_LANES])
        h = jnp.concatenate([hbuf_ref[k] for k in range(LANE_SLABS)], axis=1)
    else:
        h = h_ref[...]
    hn = _rmsnorm(h, gmix_ref[...])
    u = jnp.dot(hn.astype(BF16), win_ref[...], preferred_element_type=F32)

    up = u[:, :D_POOL]
    pbuf_ref[0:POOL_HIST_ROWS, :] = hist_ref[...]
    pbuf_ref[POOL_HIST_ROWS:, :] = up
    hist_ref[...] = up[BLOCK_ROWS - POOL_HIST_ROWS:, :]
    row = lax.broadcasted_iota(jnp.int32, (BLOCK_ROWS, POOL_GROUP), 0)
    n_pos = blk * TIME_BLOCK + row // BATCH + 1
    y_pool = []
    for gi, w in enumerate(POOL_WINDOWS):
        x = pbuf_ref[:, gi * POOL_GROUP:(gi + 1) * POOL_GROUP]
        s = x
        span = 1
        while span < w:
            s = s + pltpu.roll(s, span * BATCH, 0)
            span *= 2
        mean = s[POOL_HIST_ROWS:, :] / jnp.minimum(n_pos, w).astype(F32)
        d = mean - x[POOL_HIST_ROWS:, :]
        y_pool.append(jnp.dot(d.astype(BF16), wpool_ref[gi], preferred_element_type=F32))
    y_pool = jnp.concatenate(y_pool, axis=1) * pscale_ref[...]

    us = u[:, D_POOL:]
    usb = us.astype(BF16)
    lo = lax.broadcasted_iota(jnp.int32, (V7X_SUBLANES, STATE_PER_HALF), 0) < BATCH
    y_halves = []
    for j in range(SSM_HALVES):
        bu_ref[j] = jnp.dot(usb[:, j * CH_PER_HALF:(j + 1) * CH_PER_HALF], wb_ref[j],
                            preferred_element_type=F32)
        a_r = ar_ref[j]
        a_i = ai_ref[j]

        def body(m, v, j=j, a_r=a_r, a_i=a_i):
            r0 = pl.multiple_of(m * V7X_SUBLANES, V7X_SUBLANES)
            re = bu_ref[j, pl.ds(r0, V7X_SUBLANES), 0:STATE_PER_HALF]
            im = bu_ref[j, pl.ds(r0, V7X_SUBLANES), STATE_PER_HALF:]
            x0 = jnp.where(lo, re, pltpu.roll(im, BATCH, 0))
            x1 = jnp.where(lo, pltpu.roll(re, BATCH, 0), im)
            v0 = a_r * v + a_i * pltpu.roll(v, BATCH, 0) + x0
            v1 = a_r * v0 + a_i * pltpu.roll(v0, BATCH, 0) + x1
            bu_ref[j, pl.ds(r0, V7X_SUBLANES), 0:STATE_PER_HALF] = jnp.where(
                lo, v0, pltpu.roll(v1, BATCH, 0))
            bu_ref[j, pl.ds(r0, V7X_SUBLANES), STATE_PER_HALF:] = jnp.where(
                lo, pltpu.roll(v0, BATCH, 0), v1)
            return v1

        state_ref[j] = lax.fori_loop(0, BLOCK_ROWS // V7X_SUBLANES, body, state_ref[j], unroll=2)
        y_halves.append(jnp.dot(bu_ref[j].astype(BF16), wc_ref[j], preferred_element_type=F32))
    y = jnp.concatenate(y_halves, axis=1) + dskip_ref[...] * us
    y = 0.5 * y * (1.0 + jnp.tanh(math.sqrt(2.0 / math.pi) * (y + 0.044715 * (y * y * y))))
    z = jnp.dot(y.astype(BF16), wglu_ref[...], preferred_element_type=F32) + bglu_ref[...]
    y_ssm = y * _sigmoid(z)

    cat = jnp.concatenate([y_pool, y_ssm], axis=1).astype(BF16)
    o_ref[...] = h + jnp.dot(cat, wout_ref[...], preferred_element_type=F32)


def _layer_spec(a, layer, **kwargs):
    tail = (0,) * (a.ndim - 1)
    return pl.BlockSpec((None,) + a.shape[1:], lambda i: (layer,) + tail, **kwargs)


def _mixer_layer(h, layer, consts):
    first = layer == 0
    row_spec = pl.BlockSpec((BLOCK_ROWS, D_MODEL), lambda i: (i, 0))
    if first:
        in_spec = pl.BlockSpec((BATCH, TIME_BLOCK, D_MODEL), lambda i: (0, i, 0))
    else:
        in_spec = row_spec
    return pl.pallas_call(
        functools.partial(_mixer_kernel, first=first),
        grid=(N_BLOCKS,),
        in_specs=[in_spec] + [_layer_spec(c, layer) for c in consts],
        out_specs=row_spec,
        out_shape=jax.ShapeDtypeStruct((SEQ * BATCH, D_MODEL), F32),
        scratch_shapes=[
            pltpu.VMEM((POOL_HIST_ROWS, D_POOL), F32),
            pltpu.VMEM((SSM_HALVES, V7X_SUBLANES, STATE_PER_HALF), F32),
            pltpu.VMEM((BLOCK_ROWS + POOL_HIST_ROWS, D_POOL), F32),
            pltpu.VMEM((SSM_HALVES, BLOCK_ROWS, RE_IM * STATE_PER_HALF), F32),
            pltpu.VMEM((LANE_SLABS, BLOCK_ROWS, V7X_LANES), F32),
        ],
        compiler_params=pltpu.CompilerParams(
            dimension_semantics=("arbitrary",), vmem_limit_bytes=V7X_VMEM_LIMIT_BYTES),
        name="mixer_first" if first else "mixer",
    )(h, *consts)


def _ffn_kernel(h_ref, gffn_ref, wg_ref, wu_ref, wd_ref, gfin_ref, o_ref, obuf_ref, *, final):
    h = h_ref[...]
    hn = _rmsnorm(h, gffn_ref[...]).astype(BF16)
    g = jnp.dot(hn, wg_ref[...], preferred_element_type=F32)
    u = jnp.dot(hn, wu_ref[...], preferred_element_type=F32)
    a = (g * _sigmoid(g) * u).astype(BF16)
    out = h + jnp.dot(a, wd_ref[...], preferred_element_type=F32)
    if final:
        out = _rmsnorm(out, gfin_ref[...])
        for k in range(LANE_SLABS):
            obuf_ref[k] = out[:, k * V7X_LANES:(k + 1) * V7X_LANES]
            for b in range(BATCH):
                o_ref[b, :, k * V7X_LANES:(k + 1) * V7X_LANES] = (
                    obuf_ref[k, pl.ds(b, TIME_BLOCK, stride=BATCH), :])
    else:
        o_ref[...] = out


def _ffn_layer(h, layer, gffn, wg, wu, wd, gfin):
    final = layer == DEPTH - 1
    row_spec = pl.BlockSpec((BLOCK_ROWS, D_MODEL), lambda i: (i, 0))
    single = dict(pipeline_mode=pl.Buffered(1))
    if final:
        out_spec = pl.BlockSpec((BATCH, TIME_BLOCK, D_MODEL), lambda i: (0, i, 0))
        out_shape = jax.ShapeDtypeStruct((BATCH, SEQ, D_MODEL), F32)
    else:
        out_spec = row_spec
        out_shape = jax.ShapeDtypeStruct((SEQ * BATCH, D_MODEL), F32)
    return pl.pallas_call(
        functools.partial(_ffn_kernel, final=final),
        grid=(N_BLOCKS,),
        in_specs=[row_spec, _layer_spec(gffn, layer, **single), _layer_spec(wg, layer, **single),
                  _layer_spec(wu, layer, **single), _layer_spec(wd, layer, **single),
                  pl.BlockSpec(gfin.shape, lambda i: (0, 0), **single)],
        out_specs=out_spec,
        out_shape=out_shape,
        scratch_shapes=[pltpu.VMEM((LANE_SLABS, BLOCK_ROWS, V7X_LANES), F32)],
        compiler_params=pltpu.CompilerParams(
            dimension_semantics=("parallel",), vmem_limit_bytes=V7X_VMEM_LIMIT_BYTES),
        name="ffn_final" if final else "ffn",
    )(h, gffn, wg, wu, wd, gfin)


def _block_diag(a):
    a = a.reshape(DEPTH, SSM_HALVES, CH_PER_HALF, 1, SSM_STATE)
    a = jnp.broadcast_to(a, (DEPTH, SSM_HALVES, CH_PER_HALF, GROUPS_PER_HALF, SSM_STATE))
    row_group = lax.broadcasted_iota(jnp.int32, a.shape, 2) // SSM_GROUP
    col_group = lax.broadcasted_iota(jnp.int32, a.shape, 3)
    a = jnp.where(row_group == col_group, a, 0.0)
    return a.reshape(DEPTH, SSM_HALVES, CH_PER_HALF, STATE_PER_HALF)


def kernel(x, norm_mix, w_in, w_pool, pool_scale, lam_re, lam_im, log_dt, b_re, b_im, c_re, c_im,
           d_skip, w_glu, b_glu, w_out, norm_ffn, w_gate, w_up, w_down, norm_final):
    assert x.shape == (BATCH, SEQ, D_MODEL) and x.dtype == F32

    abar_r, abar_i, bbar_r, bbar_i = _ssm_prep(lam_re, lam_im, log_dt, b_re, b_im)
    per_group = (DEPTH, N_SSM_GROUPS, SSM_GROUP, SSM_STATE)
    abar_r = abar_r.reshape(per_group)[:, :, 0, :].reshape(DEPTH, SSM_HALVES, 1, STATE_PER_HALF)
    abar_i = abar_i.reshape(per_group)[:, :, 0, :].reshape(DEPTH, SSM_HALVES, 1, STATE_PER_HALF)
    a_r = jnp.broadcast_to(abar_r, (DEPTH, SSM_HALVES, V7X_SUBLANES, STATE_PER_HALF))
    sign = jnp.where(jnp.arange(V7X_SUBLANES) < BATCH, -1.0, 1.0).astype(F32)[None, None, :, None]
    a_i = sign * abar_i
    wb = jnp.concatenate([_block_diag(bbar_r), _block_diag(bbar_i)], axis=-1).astype(BF16)
    rows = DEPTH * N_SSM_GROUPS * SSM_GROUP
    wc = jnp.concatenate([_block_diag(c_re.reshape(rows, SSM_STATE)),
                          _block_diag(-c_im.reshape(rows, SSM_STATE))], axis=-1)
    wc = jnp.swapaxes(wc, 2, 3).astype(BF16)

    mixer_consts = (
        norm_mix.reshape(DEPTH, 1, D_MODEL), w_in.astype(BF16), w_pool.astype(BF16),
        pool_scale.reshape(DEPTH, 1, D_POOL), wb, a_r, a_i, wc, d_skip.reshape(DEPTH, 1, D_SSM),
        w_glu.astype(BF16), b_glu.reshape(DEPTH, 1, D_SSM), w_out.astype(BF16))
    gffn = norm_ffn.reshape(DEPTH, 1, D_MODEL)
    wg = w_gate.astype(BF16)
    wu = w_up.astype(BF16)
    wd = w_down.astype(BF16)
    gfin = norm_final.reshape(1, D_MODEL)

    h = x
    for layer in range(DEPTH):
        h = _mixer_layer(h, layer, mixer_consts)
        h = _ffn_layer(h, layer, gffn, wg, wu, wd, gfin)
    return h
```

```python
import functools
import math

import jax
import jax.numpy as jnp
from jax import lax
from jax.experimental import pallas as pl
from jax.experimental.pallas import tpu as pltpu

F32 = jnp.float32
BF16 = jnp.bfloat16

D_MODEL = 1024
BATCH = 4
SEQ = 4096
DEPTH = 4
D_POOL = 512
D_SSM = 512
POOL_WINDOWS = (2, 4, 8, 16)
POOL_GROUP = 128
SSM_GROUP = 16
N_SSM_GROUPS = 32
SSM_STATE = 64
D_FF = 2816
RMS_EPS = 1e-6

V7X_SUBLANES = 8
V7X_LANES = 128
V7X_MXU_DIM = 256
V7X_VMEM_LIMIT_BYTES = 56 * 1024 * 1024

SSM_HALVES = D_SSM // V7X_MXU_DIM
GROUPS_PER_HALF = N_SSM_GROUPS // SSM_HALVES
CH_PER_HALF = GROUPS_PER_HALF * SSM_GROUP
STATE_PER_HALF = GROUPS_PER_HALF * SSM_STATE
RE_IM = 2

TIME_BLOCK = 128
BLOCK_ROWS = TIME_BLOCK * BATCH
SUB_BLOCKS = 2
STEP_TIME = SUB_BLOCKS * TIME_BLOCK
STEP_ROWS = SUB_BLOCKS * BLOCK_ROWS
N_STEPS = SEQ // STEP_TIME
POOL_HIST_STEPS = max(POOL_WINDOWS)
POOL_HIST_ROWS = POOL_HIST_STEPS * BATCH
N_BLOCKS = SEQ // TIME_BLOCK
LANE_SLABS = D_MODEL // V7X_LANES

assert BATCH * RE_IM == V7X_SUBLANES, "the scan tile layout needs batch*2 == 8 sublanes"


def _rmsnorm(x, g):
    return x * lax.rsqrt(jnp.mean(x * x, axis=-1, keepdims=True) + RMS_EPS) * g


def _sigmoid(x):
    return 1.0 / (1.0 + jnp.exp(-x))


def _ssm_prep_kernel(lr_ref, li_ref, ldt_ref, br_ref, bi_ref, ar_ref, ai_ref, bbr_ref, bbi_ref):
    lr = lr_ref[...]
    li = li_ref[...]
    dt = jnp.exp(ldt_ref[...])
    mag = jnp.exp(lr * dt)
    abar_r = mag * jnp.cos(li * dt)
    abar_i = mag * jnp.sin(li * dt)
    den = lr * lr + li * li
    nr = abar_r - 1.0
    ni = abar_i
    coef_r = (nr * lr + ni * li) / den
    coef_i = (ni * lr - nr * li) / den
    br = br_ref[...]
    bi = bi_ref[...]
    ar_ref[...] = abar_r
    ai_ref[...] = abar_i
    bbr_ref[...] = coef_r * br - coef_i * bi
    bbi_ref[...] = coef_r * bi + coef_i * br


def _ssm_prep(lam_re, lam_im, log_dt, b_re, b_im):
    rows = DEPTH * N_SSM_GROUPS * SSM_GROUP

    def per_channel(a):
        a = jnp.broadcast_to(a[:, :, None, :], (DEPTH, N_SSM_GROUPS, SSM_GROUP, SSM_STATE))
        return a.reshape(rows, SSM_STATE)

    lr = per_channel(lam_re)
    li = per_channel(lam_im)
    ldt = per_channel(jnp.broadcast_to(log_dt[..., None], lam_re.shape))
    br = jnp.transpose(b_re, (0, 1, 3, 2)).reshape(rows, SSM_STATE)
    bi = jnp.transpose(b_im, (0, 1, 3, 2)).reshape(rows, SSM_STATE)
    out = jax.ShapeDtypeStruct((rows, SSM_STATE), F32)
    return pl.pallas_call(
        _ssm_prep_kernel,
        out_shape=(out, out, out, out),
        name="ssm_prep",
    )(lr, li, ldt, br, bi)


def _pool_mixer(pbuf_ref, sb, time0, wpool_ref, pscale_ref):
    row = lax.broadcasted_iota(jnp.int32, (BLOCK_ROWS, POOL_GROUP), 0)
    n_pos = time0 + row // BATCH + 1
    y_pool = []
    for gi, w in enumerate(POOL_WINDOWS):
        x = pbuf_ref[sb, :, gi * POOL_GROUP:(gi + 1) * POOL_GROUP]
        s = x
        span = 1
        while span < w:
            s = s + pltpu.roll(s, span * BATCH, 0)
            span *= 2
        mean = s[POOL_HIST_ROWS:, :] / jnp.minimum(n_pos, w).astype(F32)
        d = mean - x[POOL_HIST_ROWS:, :]
        y_pool.append(jnp.dot(d.astype(BF16), wpool_ref[gi], preferred_element_type=F32))
    return jnp.concatenate(y_pool, axis=1) * pscale_ref[...]


def _scan_block(bu_ref, sb, j, v, a_r, a_i, lo):
    for m in range(BLOCK_ROWS // V7X_SUBLANES):
        rows = slice(m * V7X_SUBLANES, (m + 1) * V7X_SUBLANES)
        re = bu_ref[sb, j, rows, 0:STATE_PER_HALF]
        im = bu_ref[sb, j, rows, STATE_PER_HALF:]
        x0 = jnp.where(lo, re, pltpu.roll(im, BATCH, 0))
        x1 = jnp.where(lo, pltpu.roll(re, BATCH, 0), im)
        v0 = a_r * v + a_i * pltpu.roll(v, BATCH, 0) + x0
        v = a_r * v0 + a_i * pltpu.roll(v0, BATCH, 0) + x1
        bu_ref[sb, j, rows, 0:STATE_PER_HALF] = jnp.where(lo, v0, pltpu.roll(v, BATCH, 0))
        bu_ref[sb, j, rows, STATE_PER_HALF:] = jnp.where(lo, pltpu.roll(v0, BATCH, 0), v)
    return v


def _mixer_kernel(h_ref, gmix_ref, win_ref, wpool_ref, pscale_ref, wb_ref, ar_ref, ai_ref,
                  wc_ref, dskip_ref, wglu_ref, bglu_ref, wout_ref, o_ref,
                  hist_ref, state_ref, pbuf_ref, bu_ref, *maybe_hbuf_ref, first):
    step = pl.program_id(0)

    @pl.when(step == 0)
    def _():
        hist_ref[...] = jnp.zeros_like(hist_ref)
        state_ref[...] = jnp.zeros_like(state_ref)

    if first:
        (hbuf_ref,) = maybe_hbuf_ref
        for k in range(LANE_SLABS):
            for b in range(BATCH):
                hbuf_ref[k, pl.ds(b, STEP_TIME, stride=BATCH), :] = (
                    h_ref[b, :, k * V7X_LANES:(k + 1) * V7X_LANES])

    lo = lax.broadcasted_iota(jnp.int32, (V7X_SUBLANES, STATE_PER_HALF), 0) < BATCH
    a_r = [ar_ref[j] for j in range(SSM_HALVES)]
    a_i = [ai_ref[j] for j in range(SSM_HALVES)]
    v = [state_ref[j] for j in range(SSM_HALVES)]
    hist = hist_ref[...]
    for sb in range(SUB_BLOCKS):
        rows = slice(sb * BLOCK_ROWS, (sb + 1) * BLOCK_ROWS)
        if first:
            h = jnp.concatenate([hbuf_ref[k, rows, :] for k in range(LANE_SLABS)], axis=1)
        else:
            h = h_ref[rows, :]
        hn = _rmsnorm(h, gmix_ref[...])
        u = jnp.dot(hn.astype(BF16), win_ref[...], preferred_element_type=F32)

        up = u[:, :D_POOL]
        pbuf_ref[sb, 0:POOL_HIST_ROWS, :] = hist
        pbuf_ref[sb, POOL_HIST_ROWS:, :] = up
        hist = up[BLOCK_ROWS - POOL_HIST_ROWS:, :]
        y_pool = _pool_mixer(pbuf_ref, sb, step * STEP_TIME + sb * TIME_BLOCK, wpool_ref, pscale_ref)

        us = u[:, D_POOL:]
        usb = us.astype(BF16)
        y_halves = []
        for j in range(SSM_HALVES):
            bu_ref[sb, j] = jnp.dot(usb[:, j * CH_PER_HALF:(j + 1) * CH_PER_HALF], wb_ref[j],
                                    preferred_element_type=F32)
            v[j] = _scan_block(bu_ref, sb, j, v[j], a_r[j], a_i[j], lo)
            y_halves.append(jnp.dot(bu_ref[sb, j].astype(BF16), wc_ref[j],
                                    preferred_element_type=F32))
        y = jnp.concatenate(y_halves, axis=1) + dskip_ref[...] * us
        y = 0.5 * y * (1.0 + jnp.tanh(math.sqrt(2.0 / math.pi) * (y + 0.044715 * (y * y * y))))
        z = jnp.dot(y.astype(BF16), wglu_ref[...], preferred_element_type=F32) + bglu_ref[...]
        y_ssm = y * _sigmoid(z)

        cat = jnp.concatenate([y_pool, y_ssm], axis=1).astype(BF16)
        o_ref[rows, :] = h + jnp.dot(cat, wout_ref[...], preferred_element_type=F32)
    hist_ref[...] = hist
    for j in range(SSM_HALVES):
        state_ref[j] = v[j]


def _layer_spec(a, layer, **kwargs):
    tail = (0,) * (a.ndim - 1)
    return pl.BlockSpec((None,) + a.shape[1:], lambda i: (layer,) + tail, **kwargs)


def _mixer_layer(h, layer, consts):
    first = layer == 0
    row_spec = pl.BlockSpec((STEP_ROWS, D_MODEL), lambda i: (i, 0))
    single = dict(pipeline_mode=pl.Buffered(1))
    scratch = [
        pltpu.VMEM((POOL_HIST_ROWS, D_POOL), F32),
        pltpu.VMEM((SSM_HALVES, V7X_SUBLANES, STATE_PER_HALF), F32),
        pltpu.VMEM((SUB_BLOCKS, BLOCK_ROWS + POOL_HIST_ROWS, D_POOL), F32),
        pltpu.VMEM((SUB_BLOCKS, SSM_HALVES, BLOCK_ROWS, RE_IM * STATE_PER_HALF), F32),
    ]
    if first:
        in_spec = pl.BlockSpec((BATCH, STEP_TIME, D_MODEL), lambda i: (0, i, 0))
        scratch.append(pltpu.VMEM((LANE_SLABS, STEP_ROWS, V7X_LANES), F32))
    else:
        in_spec = row_spec
    return pl.pallas_call(
        functools.partial(_mixer_kernel, first=first),
        grid=(N_STEPS,),
        in_specs=[in_spec] + [_layer_spec(c, layer, **single) for c in consts],
        out_specs=row_spec,
        out_shape=jax.ShapeDtypeStruct((SEQ * BATCH, D_MODEL), F32),
        scratch_shapes=scratch,
        compiler_params=pltpu.CompilerParams(
            dimension_semantics=("arbitrary",), vmem_limit_bytes=V7X_VMEM_LIMIT_BYTES),
        name="mixer_first" if first else "mixer",
    )(h, *consts)


def _ffn_kernel(h_ref, gffn_ref, wg_ref, wu_ref, wd_ref, gfin_ref, o_ref, obuf_ref, *, final):
    h = h_ref[...]
    hn = _rmsnorm(h, gffn_ref[...]).astype(BF16)
    g = jnp.dot(hn, wg_ref[...], preferred_element_type=F32)
    u = jnp.dot(hn, wu_ref[...], preferred_element_type=F32)
    a = (g * _sigmoid(g) * u).astype(BF16)
    out = h + jnp.dot(a, wd_ref[...], preferred_element_type=F32)
    if final:
        out = _rmsnorm(out, gfin_ref[...])
        for k in range(LANE_SLABS):
            obuf_ref[k] = out[:, k * V7X_LANES:(k + 1) * V7X_LANES]
            for b in range(BATCH):
                o_ref[b, :, k * V7X_LANES:(k + 1) * V7X_LANES] = (
                    obuf_ref[k, pl.ds(b, TIME_BLOCK, stride=BATCH), :])
    else:
        o_ref[...] = out


def _ffn_layer(h, layer, gffn, wg, wu, wd, gfin):
    final = layer == DEPTH - 1
    row_spec = pl.BlockSpec((BLOCK_ROWS, D_MODEL), lambda i: (i, 0))
    single = dict(pipeline_mode=pl.Buffered(1))
    if final:
        out_spec = pl.BlockSpec((BATCH, TIME_BLOCK, D_MODEL), lambda i: (0, i, 0))
        out_shape = jax.ShapeDtypeStruct((BATCH, SEQ, D_MODEL), F32)
    else:
        out_spec = row_spec
        out_shape = jax.ShapeDtypeStruct((SEQ * BATCH, D_MODEL), F32)
    return pl.pallas_call(
        functools.partial(_ffn_kernel, final=final),
        grid=(N_BLOCKS,),
        in_specs=[row_spec, _layer_spec(gffn, layer, **single), _layer_spec(wg, layer, **single),
                  _layer_spec(wu, layer, **single), _layer_spec(wd, layer, **single),
                  pl.BlockSpec(gfin.shape, lambda i: (0, 0), **single)],
        out_specs=out_spec,
        out_shape=out_shape,
        scratch_shapes=[pltpu.VMEM((LANE_SLABS, BLOCK_ROWS, V7X_LANES), F32)],
        compiler_params=pltpu.CompilerParams(
            dimension_semantics=("parallel",), vmem_limit_bytes=V7X_VMEM_LIMIT_BYTES),
        name="ffn_final" if final else "ffn",
    )(h, gffn, wg, wu, wd, gfin)


def _block_diag(a):
    a = a.reshape(DEPTH, SSM_HALVES, CH_PER_HALF, 1, SSM_STATE)
    a = jnp.broadcast_to(a, (DEPTH, SSM_HALVES, CH_PER_HALF, GROUPS_PER_HALF, SSM_STATE))
    row_group = lax.broadcasted_iota(jnp.int32, a.shape, 2) // SSM_GROUP
    col_group = lax.broadcasted_iota(jnp.int32, a.shape, 3)
    a = jnp.where(row_group == col_group, a, 0.0)
    return a.reshape(DEPTH, SSM_HALVES, CH_PER_HALF, STATE_PER_HALF)


def kernel(x, norm_mix, w_in, w_pool, pool_scale, lam_re, lam_im, log_dt, b_re, b_im, c_re, c_im,
           d_skip, w_glu, b_glu, w_out, norm_ffn, w_gate, w_up, w_down, norm_final):
    assert x.shape == (BATCH, SEQ, D_MODEL) and x.dtype == F32

    abar_r, abar_i, bbar_r, bbar_i = _ssm_prep(lam_re, lam_im, log_dt, b_re, b_im)
    per_group = (DEPTH, N_SSM_GROUPS, SSM_GROUP, SSM_STATE)
    abar_r = abar_r.reshape(per_group)[:, :, 0, :].reshape(DEPTH, SSM_HALVES, 1, STATE_PER_HALF)
    abar_i = abar_i.reshape(per_group)[:, :, 0, :].reshape(DEPTH, SSM_HALVES, 1, STATE_PER_HALF)
    a_r = jnp.broadcast_to(abar_r, (DEPTH, SSM_HALVES, V7X_SUBLANES, STATE_PER_HALF))
    sign = jnp.where(jnp.arange(V7X_SUBLANES) < BATCH, -1.0, 1.0).astype(F32)[None, None, :, None]
    a_i = sign * abar_i
    wb = jnp.concatenate([_block_diag(bbar_r), _block_diag(bbar_i)], axis=-1).astype(BF16)
    rows = DEPTH * N_SSM_GROUPS * SSM_GROUP
    wc = jnp.concatenate([_block_diag(c_re.reshape(rows, SSM_STATE)),
                          _block_diag(-c_im.reshape(rows, SSM_STATE))], axis=-1)
    wc = jnp.swapaxes(wc, 2, 3).astype(BF16)

    mixer_consts = (
        norm_mix.reshape(DEPTH, 1, D_MODEL), w_in.astype(BF16), w_pool.astype(BF16),
        pool_scale.reshape(DEPTH, 1, D_POOL), wb, a_r, a_i, wc, d_skip.reshape(DEPTH, 1, D_SSM),
        w_glu.astype(BF16), b_glu.reshape(DEPTH, 1, D_SSM), w_out.astype(BF16))
    gffn = norm_ffn.reshape(DEPTH, 1, D_MODEL)
    wg = w_gate.astype(BF16)
    wu = w_up.astype(BF16)
    wd = w_down.astype(BF16)
    gfin = norm_final.reshape(1, D_MODEL)

    h = x
    for layer in range(DEPTH):
        h = _mixer_layer(h, layer, mixer_consts)
        h = _ffn_layer(h, layer, gffn, wg, wu, wd, gfin)
    return h
```

```python
import functools
import math

import jax
import jax.numpy as jnp
from jax import lax
from jax.experimental import pallas as pl
from jax.experimental.pallas import tpu as pltpu

F32 = jnp.float32
BF16 = jnp.bfloat16

D_MODEL = 1024
BATCH = 4
SEQ = 4096
DEPTH = 4
D_POOL = 512
D_SSM = 512
POOL_WINDOWS = (2, 4, 8, 16)
POOL_GROUP = 128
SSM_GROUP = 16
N_SSM_GROUPS = 32
SSM_STATE = 64
D_FF = 2816
RMS_EPS = 1e-6

V7X_SUBLANES = 8
V7X_LANES = 128
V7X_MXU_DIM = 256
V7X_VMEM_LIMIT_BYTES = 60 * 1024 * 1024

SSM_HALVES = D_SSM // V7X_MXU_DIM
GROUPS_PER_HALF = N_SSM_GROUPS // SSM_HALVES
CH_PER_HALF = GROUPS_PER_HALF * SSM_GROUP
STATE_PER_HALF = GROUPS_PER_HALF * SSM_STATE
RE_IM = 2

TIME_BLOCK = 64
BLOCK_ROWS = TIME_BLOCK * BATCH
SUB_BLOCKS = 2
STEP_TIME = SUB_BLOCKS * TIME_BLOCK
STEP_ROWS = SUB_BLOCKS * BLOCK_ROWS
N_STEPS = SEQ // STEP_TIME
POOL_HIST_STEPS = max(POOL_WINDOWS)
POOL_HIST_ROWS = POOL_HIST_STEPS * BATCH
LANE_SLABS = D_MODEL // V7X_LANES
FF_CHUNK_EDGES = (0, 6 * V7X_MXU_DIM, D_FF)

assert BATCH * RE_IM == V7X_SUBLANES, "the scan tile layout needs batch*2 == 8 sublanes"
assert POOL_HIST_ROWS <= BLOCK_ROWS and D_FF % V7X_MXU_DIM == 0


def _rmsnorm(x, g):
    return x * lax.rsqrt(jnp.mean(x * x, axis=-1, keepdims=True) + RMS_EPS) * g


def _sigmoid(x):
    return 1.0 / (1.0 + jnp.exp(-x))


def _ssm_prep_kernel(lr_ref, li_ref, ldt_ref, br_ref, bi_ref, ar_ref, ai_ref, bbr_ref, bbi_ref):
    lr = lr_ref[...]
    li = li_ref[...]
    dt = jnp.exp(ldt_ref[...])
    mag = jnp.exp(lr * dt)
    abar_r = mag * jnp.cos(li * dt)
    abar_i = mag * jnp.sin(li * dt)
    den = lr * lr + li * li
    nr = abar_r - 1.0
    ni = abar_i
    coef_r = (nr * lr + ni * li) / den
    coef_i = (ni * lr - nr * li) / den
    br = br_ref[...]
    bi = bi_ref[...]
    ar_ref[...] = abar_r
    ai_ref[...] = abar_i
    bbr_ref[...] = coef_r * br - coef_i * bi
    bbi_ref[...] = coef_r * bi + coef_i * br


def _ssm_prep(lam_re, lam_im, log_dt, b_re, b_im):
    rows = DEPTH * N_SSM_GROUPS * SSM_GROUP

    def per_channel(a):
        a = jnp.broadcast_to(a[:, :, None, :], (DEPTH, N_SSM_GROUPS, SSM_GROUP, SSM_STATE))
        return a.reshape(rows, SSM_STATE)

    lr = per_channel(lam_re)
    li = per_channel(lam_im)
    ldt = per_channel(jnp.broadcast_to(log_dt[..., None], lam_re.shape))
    br = jnp.transpose(b_re, (0, 1, 3, 2)).reshape(rows, SSM_STATE)
    bi = jnp.transpose(b_im, (0, 1, 3, 2)).reshape(rows, SSM_STATE)
    out = jax.ShapeDtypeStruct((rows, SSM_STATE), F32)
    return pl.pallas_call(
        _ssm_prep_kernel,
        out_shape=(out, out, out, out),
        name="ssm_prep",
    )(lr, li, ldt, br, bi)


def _pool_mixer(pbuf_ref, sb, time0, wpool_ref, pscale_ref):
    row = lax.broadcasted_iota(jnp.int32, (BLOCK_ROWS, POOL_GROUP), 0)
    n_pos = time0 + row // BATCH + 1
    y_pool = []
    for gi, w in enumerate(POOL_WINDOWS):
        x = pbuf_ref[sb, :, gi * POOL_GROUP:(gi + 1) * POOL_GROUP]
        s = x
        span = 1
        while span < w:
            s = s + pltpu.roll(s, span * BATCH, 0)
            span *= 2
        mean = s[POOL_HIST_ROWS:, :] / jnp.minimum(n_pos, w).astype(F32)
        d = mean - x[POOL_HIST_ROWS:, :]
        y_pool.append(jnp.dot(d.astype(BF16), wpool_ref[gi], preferred_element_type=F32))
    return jnp.concatenate(y_pool, axis=1) * pscale_ref[...]


def _scan_block(bu_ref, sb, j, v, a_r, a_i, lo):
    for m in range(BLOCK_ROWS // V7X_SUBLANES):
        rows = slice(m * V7X_SUBLANES, (m + 1) * V7X_SUBLANES)
        re = bu_ref[sb, j, rows, 0:STATE_PER_HALF]
        im = bu_ref[sb, j, rows, STATE_PER_HALF:]
        x0 = jnp.where(lo, re, pltpu.roll(im, BATCH, 0))
        x1 = jnp.where(lo, pltpu.roll(re, BATCH, 0), im)
        v0 = a_r * v + a_i * pltpu.roll(v, BATCH, 0) + x0
        v = a_r * v0 + a_i * pltpu.roll(v0, BATCH, 0) + x1
        bu_ref[sb, j, rows, 0:STATE_PER_HALF] = jnp.where(lo, v0, pltpu.roll(v, BATCH, 0))
        bu_ref[sb, j, rows, STATE_PER_HALF:] = jnp.where(lo, pltpu.roll(v0, BATCH, 0), v)
    return v


def _ffn_block(h, gffn_ref, wg_ref, wu_ref, wd_ref):
    hn = _rmsnorm(h, gffn_ref[...]).astype(BF16)
    out = h
    for lo_col, hi_col in zip(FF_CHUNK_EDGES[:-1], FF_CHUNK_EDGES[1:]):
        g = jnp.dot(hn, wg_ref[:, lo_col:hi_col], preferred_element_type=F32)
        u = jnp.dot(hn, wu_ref[:, lo_col:hi_col], preferred_element_type=F32)
        a = (g * _sigmoid(g) * u).astype(BF16)
        out = out + jnp.dot(a, wd_ref[lo_col:hi_col, :], preferred_element_type=F32)
    return out


def _layer_kernel(h_ref, gmix_ref, win_ref, wpool_ref, pscale_ref, wb_ref, ar_ref, ai_ref,
                  wc_ref, dskip_ref, wglu_ref, bglu_ref, wout_ref,
                  gffn_ref, wg_ref, wu_ref, wd_ref, gfin_ref, o_ref,
                  hist_ref, state_ref, pbuf_ref, bu_ref, *reorder_ref, first, final):
    step = pl.program_id(0)

    @pl.when(step == 0)
    def _():
        hist_ref[...] = jnp.zeros_like(hist_ref)
        state_ref[...] = jnp.zeros_like(state_ref)

    if first:
        hbuf_ref = reorder_ref[0]
        for k in range(LANE_SLABS):
            for b in range(BATCH):
                hbuf_ref[k, pl.ds(b, STEP_TIME, stride=BATCH), :] = (
                    h_ref[b, :, k * V7X_LANES:(k + 1) * V7X_LANES])

    lo = lax.broadcasted_iota(jnp.int32, (V7X_SUBLANES, STATE_PER_HALF), 0) < BATCH
    a_r = [ar_ref[j] for j in range(SSM_HALVES)]
    a_i = [ai_ref[j] for j in range(SSM_HALVES)]
    v = [state_ref[j] for j in range(SSM_HALVES)]
    hist = hist_ref[...]
    heads = []
    for sb in range(SUB_BLOCKS):
        rows = slice(sb * BLOCK_ROWS, (sb + 1) * BLOCK_ROWS)
        if first:
            h = jnp.concatenate([hbuf_ref[k, rows, :] for k in range(LANE_SLABS)], axis=1)
        else:
            h = h_ref[rows, :]
        hn = _rmsnorm(h, gmix_ref[...])
        u = jnp.dot(hn.astype(BF16), win_ref[...], preferred_element_type=F32)

        us = u[:, D_POOL:]
        usb = us.astype(BF16)
        for j in range(SSM_HALVES):
            bu_ref[sb, j] = jnp.dot(usb[:, j * CH_PER_HALF:(j + 1) * CH_PER_HALF], wb_ref[j],
                                    preferred_element_type=F32)

        up = u[:, :D_POOL]
        pbuf_ref[sb, 0:POOL_HIST_ROWS, :] = hist
        pbuf_ref[sb, POOL_HIST_ROWS:, :] = up
        hist = up[BLOCK_ROWS - POOL_HIST_ROWS:, :]
        y_pool = _pool_mixer(pbuf_ref, sb, step * STEP_TIME + sb * TIME_BLOCK, wpool_ref, pscale_ref)
        heads.append((h, us, y_pool))

    for sb in range(SUB_BLOCKS):
        for j in range(SSM_HALVES):
            v[j] = _scan_block(bu_ref, sb, j, v[j], a_r[j], a_i[j], lo)

    for sb in range(SUB_BLOCKS):
        rows = slice(sb * BLOCK_ROWS, (sb + 1) * BLOCK_ROWS)
        h, us, y_pool = heads[sb]
        y_halves = [jnp.dot(bu_ref[sb, j].astype(BF16), wc_ref[j], preferred_element_type=F32)
                    for j in range(SSM_HALVES)]
        y = jnp.concatenate(y_halves, axis=1) + dskip_ref[...] * us
        y = 0.5 * y * (1.0 + jnp.tanh(math.sqrt(2.0 / math.pi) * (y + 0.044715 * (y * y * y))))
        z = jnp.dot(y.astype(BF16), wglu_ref[...], preferred_element_type=F32) + bglu_ref[...]
        y_ssm = y * _sigmoid(z)

        cat = jnp.concatenate([y_pool, y_ssm], axis=1).astype(BF16)
        h = h + jnp.dot(cat, wout_ref[...], preferred_element_type=F32)

        out = _ffn_block(h, gffn_ref, wg_ref, wu_ref, wd_ref)
        if final:
            obuf_ref = reorder_ref[-1]
            out = _rmsnorm(out, gfin_ref[...])
            t0 = sb * TIME_BLOCK
            for k in range(LANE_SLABS):
                obuf_ref[sb, k] = out[:, k * V7X_LANES:(k + 1) * V7X_LANES]
                for b in range(BATCH):
                    o_ref[b, t0:t0 + TIME_BLOCK, k * V7X_LANES:(k + 1) * V7X_LANES] = (
                        obuf_ref[sb, k, pl.ds(b, TIME_BLOCK, stride=BATCH), :])
        else:
            o_ref[rows, :] = out
    hist_ref[...] = hist
    for j in range(SSM_HALVES):
        state_ref[j] = v[j]


def _layer_spec(a, layer):
    tail = (0,) * (a.ndim - 1)
    return pl.BlockSpec((None,) + a.shape[1:], lambda i: (layer,) + tail,
                        pipeline_mode=pl.Buffered(1))


def _layer(h, layer, consts, gfin):
    first = layer == 0
    final = layer == DEPTH - 1
    row_spec = pl.BlockSpec((STEP_ROWS, D_MODEL), lambda i: (i, 0))
    btd_spec = pl.BlockSpec((BATCH, STEP_TIME, D_MODEL), lambda i: (0, i, 0))
    scratch = [
        pltpu.VMEM((POOL_HIST_ROWS, D_POOL), F32),
        pltpu.VMEM((SSM_HALVES, V7X_SUBLANES, STATE_PER_HALF), F32),
        pltpu.VMEM((SUB_BLOCKS, BLOCK_ROWS + POOL_HIST_ROWS, D_POOL), F32),
        pltpu.VMEM((SUB_BLOCKS, SSM_HALVES, BLOCK_ROWS, RE_IM * STATE_PER_HALF), F32),
    ]
    if first:
        scratch.append(pltpu.VMEM((LANE_SLABS, STEP_ROWS, V7X_LANES), F32))
    if final:
        scratch.append(pltpu.VMEM((SUB_BLOCKS, LANE_SLABS, BLOCK_ROWS, V7X_LANES), F32))
    if final:
        out_shape = jax.ShapeDtypeStruct((BATCH, SEQ, D_MODEL), F32)
    else:
        out_shape = jax.ShapeDtypeStruct((SEQ * BATCH, D_MODEL), F32)
    return pl.pallas_call(
        functools.partial(_layer_kernel, first=first, final=final),
        grid=(N_STEPS,),
        in_specs=([btd_spec if first else row_spec] + [_layer_spec(c, layer) for c in consts]
                  + [pl.BlockSpec(gfin.shape, lambda i: (0, 0), pipeline_mode=pl.Buffered(1))]),
        out_specs=btd_spec if final else row_spec,
        out_shape=out_shape,
        scratch_shapes=scratch,
        compiler_params=pltpu.CompilerParams(
            dimension_semantics=("arbitrary",), vmem_limit_bytes=V7X_VMEM_LIMIT_BYTES),
        name="layer_%d" % layer,
    )(h, *consts, gfin)


def _block_diag(a):
    a = a.reshape(DEPTH, SSM_HALVES, CH_PER_HALF, 1, SSM_STATE)
    a = jnp.broadcast_to(a, (DEPTH, SSM_HALVES, CH_PER_HALF, GROUPS_PER_HALF, SSM_STATE))
    row_group = lax.broadcasted_iota(jnp.int32, a.shape, 2) // SSM_GROUP
    col_group = lax.broadcasted_iota(jnp.int32, a.shape, 3)
    a = jnp.where(row_group == col_group, a, 0.0)
    return a.reshape(DEPTH, SSM_HALVES, CH_PER_HALF, STATE_PER_HALF)


def kernel(x, norm_mix, w_in, w_pool, pool_scale, lam_re, lam_im, log_dt, b_re, b_im, c_re, c_im,
           d_skip, w_glu, b_glu, w_out, norm_ffn, w_gate, w_up, w_down, norm_final):
    assert x.shape == (BATCH, SEQ, D_MODEL) and x.dtype == F32

    abar_r, abar_i, bbar_r, bbar_i = _ssm_prep(lam_re, lam_im, log_dt, b_re, b_im)
    per_group = (DEPTH, N_SSM_GROUPS, SSM_GROUP, SSM_STATE)
    abar_r = abar_r.reshape(per_group)[:, :, 0, :].reshape(DEPTH, SSM_HALVES, 1, STATE_PER_HALF)
    abar_i = abar_i.reshape(per_group)[:, :, 0, :].reshape(DEPTH, SSM_HALVES, 1, STATE_PER_HALF)
    a_r = jnp.broadcast_to(abar_r, (DEPTH, SSM_HALVES, V7X_SUBLANES, STATE_PER_HALF))
    sign = jnp.where(jnp.arange(V7X_SUBLANES) < BATCH, -1.0, 1.0).astype(F32)[None, None, :, None]
    a_i = sign * abar_i
    wb = jnp.concatenate([_block_diag(bbar_r), _block_diag(bbar_i)], axis=-1).astype(BF16)
    rows = DEPTH * N_SSM_GROUPS * SSM_GROUP
    wc = jnp.concatenate([_block_diag(c_re.reshape(rows, SSM_STATE)),
                          _block_diag(-c_im.reshape(rows, SSM_STATE))], axis=-1)
    wc = jnp.swapaxes(wc, 2, 3).astype(BF16)

    consts = (
        norm_mix.reshape(DEPTH, 1, D_MODEL), w_in.astype(BF16), w_pool.astype(BF16),
        pool_scale.reshape(DEPTH, 1, D_POOL), wb, a_r, a_i, wc, d_skip.reshape(DEPTH, 1, D_SSM),
        w_glu.astype(BF16), b_glu.reshape(DEPTH, 1, D_SSM), w_out.astype(BF16),
        norm_ffn.reshape(DEPTH, 1, D_MODEL), w_gate.astype(BF16), w_up.astype(BF16),
        w_down.astype(BF16))
    gfin = norm_final.reshape(1, D_MODEL)

    h = x
    for layer in range(DEPTH):
        h = _layer(h, layer, consts, gfin)
    return h
```

```python
import functools
import math

import jax
import jax.numpy as jnp
from jax import lax
from jax.experimental import pallas as pl
from jax.experimental.pallas import tpu as pltpu

F32 = jnp.float32
BF16 = jnp.bfloat16

D_MODEL = 1024
BATCH = 4
SEQ = 4096
DEPTH = 4
D_POOL = 512
D_SSM = 512
POOL_WINDOWS = (2, 4, 8, 16)
POOL_GROUP = 128
SSM_GROUP = 16
N_SSM_GROUPS = 32
SSM_STATE = 64
D_FF = 2816
RMS_EPS = 1e-6

V7X_SUBLANES = 8
V7X_LANES = 128
V7X_MXU_DIM = 256
V7X_VMEM_LIMIT_BYTES = 60 * 1024 * 1024

SSM_HALVES = D_SSM // V7X_MXU_DIM
GROUPS_PER_HALF = N_SSM_GROUPS // SSM_HALVES
CH_PER_HALF = GROUPS_PER_HALF * SSM_GROUP
STATE_PER_HALF = GROUPS_PER_HALF * SSM_STATE
RE_IM = 2
SCAN_LANES = 4 * V7X_LANES

TIME_BLOCK = 64
BLOCK_ROWS = TIME_BLOCK * BATCH
SUB_BLOCKS = 2
STEP_TIME = SUB_BLOCKS * TIME_BLOCK
STEP_ROWS = SUB_BLOCKS * BLOCK_ROWS
N_STEPS = SEQ // STEP_TIME
POOL_HIST_STEPS = max(POOL_WINDOWS)
POOL_HIST_ROWS = POOL_HIST_STEPS * BATCH
LANE_SLABS = D_MODEL // V7X_LANES
FF_CHUNK_EDGES = (0, 6 * V7X_MXU_DIM, D_FF)
BIG_WEIGHTS = ("w_in", "w_pool", "w_glu", "w_out", "w_gate", "w_up", "w_down")
N_SMALL_PARAMS = 9
CAST_BLOCK_ROWS = 128

assert BATCH * RE_IM == V7X_SUBLANES, "the scan tile layout needs batch*2 == 8 sublanes"
assert POOL_HIST_ROWS <= BLOCK_ROWS and D_FF % V7X_MXU_DIM == 0


def _rmsnorm(x, g):
    return x * lax.rsqrt(jnp.mean(x * x, axis=-1, keepdims=True) + RMS_EPS) * g


def _sigmoid(x):
    return 1.0 / (1.0 + jnp.exp(-x))


def _ssm_prep_kernel(lr_ref, li_ref, ldt_ref, br_ref, bi_ref, ctr_ref, cti_ref,
                     ar_ref, ai_ref, wb_ref, wc_ref):
    lr = lr_ref[...]
    li = li_ref[...]
    dt = jnp.exp(ldt_ref[...])
    mag = jnp.exp(lr * dt)
    abar_r = mag * jnp.cos(li * dt)
    abar_i = mag * jnp.sin(li * dt)
    den = lr * lr + li * li
    nr = abar_r - 1.0
    ni = abar_i
    coef_r = (nr * lr + ni * li) / den
    coef_i = (ni * lr - nr * li) / den
    br = br_ref[...]
    bi = bi_ref[...]
    ar_ref[...] = abar_r
    ai_ref[...] = abar_i
    bbar_r = coef_r * br - coef_i * bi
    bbar_i = coef_r * bi + coef_i * br

    def iota(shape, axis):
        return lax.broadcasted_iota(jnp.int32, shape, axis)

    tile_in = (iota((SSM_STATE, STATE_PER_HALF), 1) % SSM_STATE
               == iota((SSM_STATE, STATE_PER_HALF), 0)).astype(BF16)
    same_group_in = (iota((CH_PER_HALF, STATE_PER_HALF), 0) // SSM_GROUP
                     == iota((CH_PER_HALF, STATE_PER_HALF), 1) // SSM_STATE)
    for part, bbar in enumerate((bbar_r, bbar_i)):
        tiled = jnp.dot(bbar.astype(BF16), tile_in, preferred_element_type=F32)
        wb_ref[:, part * STATE_PER_HALF:(part + 1) * STATE_PER_HALF] = (
            jnp.where(same_group_in, tiled, 0.0).astype(BF16))

    tile_out = (iota((STATE_PER_HALF, SSM_STATE), 0) % SSM_STATE
                == iota((STATE_PER_HALF, SSM_STATE), 1)).astype(BF16)
    same_group_out = (iota((STATE_PER_HALF, CH_PER_HALF), 0) // SSM_STATE
                      == iota((STATE_PER_HALF, CH_PER_HALF), 1) // SSM_GROUP)
    for part, ct in enumerate((ctr_ref[...], -cti_ref[...])):
        tiled = jnp.dot(tile_out, ct.astype(BF16), preferred_element_type=F32)
        wc_ref[part * STATE_PER_HALF:(part + 1) * STATE_PER_HALF, :] = (
            jnp.where(same_group_out, tiled, 0.0).astype(BF16))


def _ssm_prep(lam_re, lam_im, log_dt, b_re, b_im, c_re, c_im):
    n_half = DEPTH * SSM_HALVES

    def per_channel(a):
        a = jnp.broadcast_to(a[:, :, None, :], (DEPTH, N_SSM_GROUPS, SSM_GROUP, SSM_STATE))
        return a.reshape(n_half, CH_PER_HALF, SSM_STATE)

    def channel_rows(a):
        return jnp.transpose(a, (0, 1, 3, 2)).reshape(n_half, CH_PER_HALF, SSM_STATE)

    def state_rows(a):
        return jnp.transpose(a.reshape(n_half, CH_PER_HALF, SSM_STATE), (0, 2, 1))

    ins = (per_channel(lam_re), per_channel(lam_im),
           per_channel(jnp.broadcast_to(log_dt[..., None], lam_re.shape)),
           channel_rows(b_re), channel_rows(b_im), state_rows(c_re), state_rows(c_im))

    def spec(a):
        return pl.BlockSpec((None,) + a.shape[1:], lambda i: (i, 0, 0))

    outs = (jax.ShapeDtypeStruct((n_half, CH_PER_HALF, SSM_STATE), F32),
            jax.ShapeDtypeStruct((n_half, CH_PER_HALF, SSM_STATE), F32),
            jax.ShapeDtypeStruct((n_half, CH_PER_HALF, RE_IM * STATE_PER_HALF), BF16),
            jax.ShapeDtypeStruct((n_half, RE_IM * STATE_PER_HALF, CH_PER_HALF), BF16))
    return pl.pallas_call(
        _ssm_prep_kernel,
        grid=(n_half,),
        in_specs=[spec(a) for a in ins],
        out_specs=tuple(spec(o) for o in outs),
        out_shape=outs,
        compiler_params=pltpu.CompilerParams(dimension_semantics=("parallel",)),
        name="ssm_prep",
    )(*ins)


def _pool_mixer(pbuf_ref, sb, time0, wpool_ref, pscale_ref):
    row = lax.broadcasted_iota(jnp.int32, (BLOCK_ROWS, POOL_GROUP), 0)
    n_pos = time0 + row // BATCH + 1
    y_pool = []
    for gi, w in enumerate(POOL_WINDOWS):
        x = pbuf_ref[sb, :, gi * POOL_GROUP:(gi + 1) * POOL_GROUP]
        s = x
        span = 1
        while span < w:
            s = s + pltpu.roll(s, span * BATCH, 0)
            span *= 2
        mean = s[POOL_HIST_ROWS:, :] / jnp.minimum(n_pos, w).astype(F32)
        d = mean - x[POOL_HIST_ROWS:, :]
        y_pool.append(jnp.dot(d.astype(BF16), wpool_ref[gi * POOL_GROUP:(gi + 1) * POOL_GROUP, :],
                              preferred_element_type=F32))
    return jnp.concatenate(y_pool, axis=1) * pscale_ref[...]


def _swap_time_pairs(x):
    n = x.shape[0]
    first_half = lax.broadcasted_iota(jnp.int32, x.shape, 0) % V7X_SUBLANES < BATCH
    return jnp.where(first_half, pltpu.roll(x, n - BATCH, 0), pltpu.roll(x, BATCH, 0))


def _scan_chain(bu_ref, sb, j, lanes, w, a_r, a_i, lo):
    im_lanes = slice(STATE_PER_HALF + lanes.start, STATE_PER_HALF + lanes.stop)
    for m in range(BLOCK_ROWS // V7X_SUBLANES):
        rows = slice(m * V7X_SUBLANES, (m + 1) * V7X_SUBLANES)
        re = bu_ref[sb, j, rows, lanes]
        im = bu_ref[sb, j, rows, im_lanes]
        x0 = jnp.where(lo, re, im)
        x1 = jnp.where(lo, im, re)
        v0 = a_r * pltpu.roll(w, BATCH, 0) + a_i * w + x0
        w = a_r * pltpu.roll(v0, BATCH, 0) - a_i * v0 + x1
        bu_ref[sb, j, rows, lanes] = jnp.where(lo, v0, w)
        bu_ref[sb, j, rows, im_lanes] = jnp.where(lo, w, v0)
    return w


def _ffn_block(h, gffn_ref, wg_ref, wu_ref, wd_ref):
    hn = _rmsnorm(h, gffn_ref[...]).astype(BF16)
    out = h
    for lo_col, hi_col in zip(FF_CHUNK_EDGES[:-1], FF_CHUNK_EDGES[1:]):
        g = jnp.dot(hn, wg_ref[:, lo_col:hi_col], preferred_element_type=F32)
        u = jnp.dot(hn, wu_ref[:, lo_col:hi_col], preferred_element_type=F32)
        a = (g * _sigmoid(g) * u).astype(BF16)
        out = out + jnp.dot(a, wd_ref[lo_col:hi_col, :], preferred_element_type=F32)
    return out


def _layer_kernel(*refs, first, final):
    refs = list(refs)
    h_ref = refs.pop(0)
    (gmix_ref, pscale_ref, wb_ref, ar_ref, ai_ref, wc_ref, dskip_ref, bglu_ref,
     gffn_ref) = [refs.pop(0) for _ in range(N_SMALL_PARAMS)]
    win_ref, wpool_ref, wglu_ref, wout_ref, wg_ref, wu_ref, wd_ref = [
        refs.pop(0) for _ in BIG_WEIGHTS]
    gfin_ref = refs.pop(0)
    next_f32 = [] if final else [refs.pop(0) for _ in BIG_WEIGHTS]
    o_ref = refs.pop(0)
    next_bf16 = [] if final else [refs.pop(0) for _ in BIG_WEIGHTS]
    hist_ref, state_ref, pbuf_ref, bu_ref = [refs.pop(0) for _ in range(4)]
    reorder_ref = refs
    step = pl.program_id(0)

    for src, dst in zip(next_f32, next_bf16):
        dst[...] = src[...].astype(BF16)

    @pl.when(step == 0)
    def _():
        hist_ref[...] = jnp.zeros_like(hist_ref)
        state_ref[...] = jnp.zeros_like(state_ref)

    lo = lax.broadcasted_iota(jnp.int32, (V7X_SUBLANES, SCAN_LANES), 0) < BATCH
    v = [[state_ref[j, :, c * SCAN_LANES:(c + 1) * SCAN_LANES]
          for c in range(STATE_PER_HALF // SCAN_LANES)] for j in range(SSM_HALVES)]
    hist = hist_ref[...]
    heads = []
    for sb in range(SUB_BLOCKS):
        rows = slice(sb * BLOCK_ROWS, (sb + 1) * BLOCK_ROWS)
        if first:
            hbuf_ref = reorder_ref[0]
            t0 = sb * TIME_BLOCK
            for k in range(LANE_SLABS):
                for b in range(BATCH):
                    hbuf_ref[k, pl.ds(b, TIME_BLOCK, stride=BATCH), :] = (
                        h_ref[b, t0:t0 + TIME_BLOCK, k * V7X_LANES:(k + 1) * V7X_LANES])
            h = jnp.concatenate([hbuf_ref[k] for k in range(LANE_SLABS)], axis=1)
        else:
            h = h_ref[rows, :]
        hn = _rmsnorm(h, gmix_ref[...])
        u = jnp.dot(hn.astype(BF16), win_ref[...], preferred_element_type=F32)

        us = u[:, D_POOL:]
        usb = us.astype(BF16)
        usb_swapped = _swap_time_pairs(us).astype(BF16)
        for j in range(SSM_HALVES):
            cols = slice(j * CH_PER_HALF, (j + 1) * CH_PER_HALF)
            bu_ref[sb, j, :, 0:STATE_PER_HALF] = jnp.dot(
                usb[:, cols], wb_ref[j, :, 0:STATE_PER_HALF], preferred_element_type=F32)
            bu_ref[sb, j, :, STATE_PER_HALF:] = jnp.dot(
                usb_swapped[:, cols], wb_ref[j, :, STATE_PER_HALF:], preferred_element_type=F32)

        up = u[:, :D_POOL]
        pbuf_ref[sb, 0:POOL_HIST_ROWS, :] = hist
        pbuf_ref[sb, POOL_HIST_ROWS:, :] = up
        hist = up[BLOCK_ROWS - POOL_HIST_ROWS:, :]
        y_pool = _pool_mixer(pbuf_ref, sb, step * STEP_TIME + sb * TIME_BLOCK, wpool_ref, pscale_ref)
        heads.append((h, us, y_pool))

    never = step < 0
    prev = None
    for sb in range(SUB_BLOCKS):
        for j in range(SSM_HALVES):
            for c in range(STATE_PER_HALF // SCAN_LANES):
                lanes = slice(c * SCAN_LANES, (c + 1) * SCAN_LANES)
                w = v[j][c]
                if prev is not None:
                    w = jnp.where(never, prev, w)
                w = _scan_chain(bu_ref, sb, j, lanes, w, ar_ref[j, :, lanes], ai_ref[j, :, lanes],
                                lo)
                v[j][c] = w
                prev = w

    for sb in range(SUB_BLOCKS):
        rows = slice(sb * BLOCK_ROWS, (sb + 1) * BLOCK_ROWS)
        h, us, y_pool = heads[sb]
        y_halves = []
        for j in range(SSM_HALVES):
            y_re = jnp.dot(bu_ref[sb, j, :, 0:STATE_PER_HALF].astype(BF16),
                           wc_ref[j, 0:STATE_PER_HALF, :], preferred_element_type=F32)
            y_im = jnp.dot(bu_ref[sb, j, :, STATE_PER_HALF:].astype(BF16),
                           wc_ref[j, STATE_PER_HALF:, :], preferred_element_type=F32)
            y_halves.append(y_re + _swap_time_pairs(y_im))
        y = jnp.concatenate(y_halves, axis=1) + dskip_ref[...] * us
        y = 0.5 * y * (1.0 + jnp.tanh(math.sqrt(2.0 / math.pi) * (y + 0.044715 * (y * y * y))))
        z = jnp.dot(y.astype(BF16), wglu_ref[...], preferred_element_type=F32) + bglu_ref[...]
        y_ssm = y * _sigmoid(z)

        cat = jnp.concatenate([y_pool, y_ssm], axis=1).astype(BF16)
        h = h + jnp.dot(cat, wout_ref[...], preferred_element_type=F32)

        out = _ffn_block(h, gffn_ref, wg_ref, wu_ref, wd_ref)
        if final:
            obuf_ref = reorder_ref[-1]
            out = _rmsnorm(out, gfin_ref[...])
            t0 = sb * TIME_BLOCK
            for k in range(LANE_SLABS):
                obuf_ref[k] = out[:, k * V7X_LANES:(k + 1) * V7X_LANES]
                for b in range(BATCH):
                    o_ref[b, t0:t0 + TIME_BLOCK, k * V7X_LANES:(k + 1) * V7X_LANES] = (
                        obuf_ref[k, pl.ds(b, TIME_BLOCK, stride=BATCH), :])
        else:
            o_ref[rows, :] = out
    hist_ref[...] = hist
    for j in range(SSM_HALVES):
        for c in range(STATE_PER_HALF // SCAN_LANES):
            state_ref[j, :, c * SCAN_LANES:(c + 1) * SCAN_LANES] = v[j][c]


def _layer_spec(a, layer):
    tail = (0,) * (a.ndim - 1)
    return pl.BlockSpec((None,) + a.shape[1:], lambda i: (layer,) + tail,
                        pipeline_mode=pl.Buffered(1))


def _resident_spec(a):
    zeros = (0,) * a.ndim
    return pl.BlockSpec(a.shape, lambda i: zeros, pipeline_mode=pl.Buffered(1))


def _cast_specs(w, layer):
    rows, cols = w.shape[1:]
    block_rows = CAST_BLOCK_ROWS if rows % N_STEPS or rows // N_STEPS % 16 else rows // N_STEPS
    last = rows // block_rows - 1
    assert rows % block_rows == 0 and last < N_STEPS
    src = pl.BlockSpec((None, block_rows, cols), lambda i: (layer, jnp.minimum(i, last), 0))
    dst = pl.BlockSpec((block_rows, cols), lambda i: (jnp.minimum(i, last), 0))
    return src, dst, jax.ShapeDtypeStruct((rows, cols), BF16)


def _layer(h, layer, small, weights, gfin, weights_f32):
    first = layer == 0
    final = layer == DEPTH - 1
    row_spec = pl.BlockSpec((STEP_ROWS, D_MODEL), lambda i: (i, 0))
    btd_spec = pl.BlockSpec((BATCH, STEP_TIME, D_MODEL), lambda i: (0, i, 0))
    scratch = [
        pltpu.VMEM((POOL_HIST_ROWS, D_POOL), F32),
        pltpu.VMEM((SSM_HALVES, V7X_SUBLANES, STATE_PER_HALF), F32),
        pltpu.VMEM((SUB_BLOCKS, BLOCK_ROWS + POOL_HIST_ROWS, D_POOL), F32),
        pltpu.VMEM((SUB_BLOCKS, SSM_HALVES, BLOCK_ROWS, RE_IM * STATE_PER_HALF), F32),
    ]
    if first:
        scratch.append(pltpu.VMEM((LANE_SLABS, BLOCK_ROWS, V7X_LANES), F32))
    if final:
        scratch.append(pltpu.VMEM((LANE_SLABS, BLOCK_ROWS, V7X_LANES), F32))
    in_specs = ([btd_spec if first else row_spec] + [_layer_spec(c, layer) for c in small]
                + [_resident_spec(w) for w in weights] + [_resident_spec(gfin)])
    operands = [h, *small, *weights, gfin]
    if final:
        out_specs = [btd_spec]
        out_shape = [jax.ShapeDtypeStruct((BATCH, SEQ, D_MODEL), F32)]
    else:
        out_specs = [row_spec]
        out_shape = [jax.ShapeDtypeStruct((SEQ * BATCH, D_MODEL), F32)]
        for w in weights_f32:
            src, dst, shape = _cast_specs(w, layer + 1)
            in_specs.append(src)
            operands.append(w)
            out_specs.append(dst)
            out_shape.append(shape)
    outs = pl.pallas_call(
        functools.partial(_layer_kernel, first=first, final=final),
        grid=(N_STEPS,),
        in_specs=in_specs,
        out_specs=out_specs,
        out_shape=out_shape,
        scratch_shapes=scratch,
        compiler_params=pltpu.CompilerParams(
            dimension_semantics=("arbitrary",), vmem_limit_bytes=V7X_VMEM_LIMIT_BYTES),
        name="layer_%d" % layer,
    )(*operands)
    return outs[0], outs[1:]


def kernel(x, norm_mix, w_in, w_pool, pool_scale, lam_re, lam_im, log_dt, b_re, b_im, c_re, c_im,
           d_skip, w_glu, b_glu, w_out, norm_ffn, w_gate, w_up, w_down, norm_final):
    assert x.shape == (BATCH, SEQ, D_MODEL) and x.dtype == F32

    abar_r, abar_i, wb, wc = _ssm_prep(lam_re, lam_im, log_dt, b_re, b_im, c_re, c_im)
    per_row = (DEPTH, SSM_HALVES, GROUPS_PER_HALF, SSM_GROUP, SSM_STATE)
    abar_r = abar_r.reshape(per_row)[:, :, :, 0, :].reshape(DEPTH, SSM_HALVES, 1, STATE_PER_HALF)
    abar_i = abar_i.reshape(per_row)[:, :, :, 0, :].reshape(DEPTH, SSM_HALVES, 1, STATE_PER_HALF)
    a_r = jnp.broadcast_to(abar_r, (DEPTH, SSM_HALVES, V7X_SUBLANES, STATE_PER_HALF))
    sign = jnp.where(jnp.arange(V7X_SUBLANES) < BATCH, -1.0, 1.0).astype(F32)[None, None, :, None]
    a_i = sign * abar_i
    wb = wb.reshape(DEPTH, SSM_HALVES, CH_PER_HALF, RE_IM * STATE_PER_HALF)
    wc = wc.reshape(DEPTH, SSM_HALVES, RE_IM * STATE_PER_HALF, CH_PER_HALF)

    small = (
        norm_mix.reshape(DEPTH, 1, D_MODEL), pool_scale.reshape(DEPTH, 1, D_POOL), wb, a_r, a_i, wc,
        d_skip.reshape(DEPTH, 1, D_SSM), b_glu.reshape(DEPTH, 1, D_SSM),
        norm_ffn.reshape(DEPTH, 1, D_MODEL))
    assert len(small) == N_SMALL_PARAMS
    weights_f32 = (w_in, w_pool.reshape(DEPTH, D_POOL, POOL_GROUP), w_glu, w_out, w_gate, w_up,
                   w_down)
    weights = [w[0].astype(BF16) for w in weights_f32]
    gfin = norm_final.reshape(1, D_MODEL)

    h = x
    for layer in range(DEPTH):
        h, weights = _layer(h, layer, small, weights, gfin, weights_f32)
    return h
```

```python
import functools
import math

import jax
import jax.numpy as jnp
from jax import lax
from jax.experimental import pallas as pl
from jax.experimental.pallas import tpu as pltpu

F32 = jnp.float32
BF16 = jnp.bfloat16

D_MODEL = 1024
BATCH = 4
SEQ = 4096
DEPTH = 4
D_POOL = 512
D_SSM = 512
POOL_WINDOWS = (2, 4, 8, 16)
POOL_GROUP = 128
SSM_GROUP = 16
N_SSM_GROUPS = 32
SSM_STATE = 64
D_FF = 2816
RMS_EPS = 1e-6

V7X_SUBLANES = 8
V7X_LANES = 128
V7X_MXU_DIM = 256
V7X_VMEM_LIMIT_BYTES = 60 * 1024 * 1024

SSM_HALVES = D_SSM // V7X_MXU_DIM
GROUPS_PER_HALF = N_SSM_GROUPS // SSM_HALVES
CH_PER_HALF = GROUPS_PER_HALF * SSM_GROUP
STATE_PER_HALF = GROUPS_PER_HALF * SSM_STATE
RE_IM = 2
SCAN_LANES = 4 * V7X_LANES

TIME_BLOCK = 64
BLOCK_ROWS = TIME_BLOCK * BATCH
SUB_BLOCKS = 2
STEP_TIME = SUB_BLOCKS * TIME_BLOCK
STEP_ROWS = SUB_BLOCKS * BLOCK_ROWS
N_STEPS = SEQ // STEP_TIME
POOL_HIST_STEPS = max(POOL_WINDOWS)
POOL_HIST_ROWS = POOL_HIST_STEPS * BATCH
LANE_SLABS = D_MODEL // V7X_LANES
FF_CHUNK_EDGES = (0, 6 * V7X_MXU_DIM, D_FF)
BIG_WEIGHTS = ("w_in", "w_pool", "w_glu", "w_out", "w_gate", "w_up", "w_down")
N_SMALL_PARAMS = 9
CAST_BLOCK_ROWS = 128

assert BATCH * RE_IM == V7X_SUBLANES, "the scan tile layout needs batch*2 == 8 sublanes"
assert POOL_HIST_ROWS <= BLOCK_ROWS and D_FF % V7X_MXU_DIM == 0


def _rmsnorm(x, g):
    return x * lax.rsqrt(jnp.mean(x * x, axis=-1, keepdims=True) + RMS_EPS) * g


def _sigmoid(x):
    return 0.5 + 0.5 * jnp.tanh(0.5 * x)


def _ssm_prep_kernel(lr_ref, li_ref, ldt_ref, br_ref, bi_ref, ctr_ref, cti_ref,
                     ar_ref, ai_ref, wb_ref, wc_ref):
    lr = lr_ref[...]
    li = li_ref[...]
    dt = jnp.exp(ldt_ref[...])
    mag = jnp.exp(lr * dt)
    abar_r = mag * jnp.cos(li * dt)
    abar_i = mag * jnp.sin(li * dt)
    den = lr * lr + li * li
    nr = abar_r - 1.0
    ni = abar_i
    coef_r = (nr * lr + ni * li) / den
    coef_i = (ni * lr - nr * li) / den
    br = br_ref[...]
    bi = bi_ref[...]
    ar_ref[...] = abar_r
    ai_ref[...] = abar_i
    bbar_r = coef_r * br - coef_i * bi
    bbar_i = coef_r * bi + coef_i * br

    def iota(shape, axis):
        return lax.broadcasted_iota(jnp.int32, shape, axis)

    tile_in = (iota((SSM_STATE, STATE_PER_HALF), 1) % SSM_STATE
               == iota((SSM_STATE, STATE_PER_HALF), 0)).astype(BF16)
    same_group_in = (iota((CH_PER_HALF, STATE_PER_HALF), 0) // SSM_GROUP
                     == iota((CH_PER_HALF, STATE_PER_HALF), 1) // SSM_STATE)
    for part, bbar in enumerate((bbar_r, bbar_i)):
        tiled = jnp.dot(bbar.astype(BF16), tile_in, preferred_element_type=F32)
        wb_ref[:, part * STATE_PER_HALF:(part + 1) * STATE_PER_HALF] = (
            jnp.where(same_group_in, tiled, 0.0).astype(BF16))

    tile_out = (iota((STATE_PER_HALF, SSM_STATE), 0) % SSM_STATE
                == iota((STATE_PER_HALF, SSM_STATE), 1)).astype(BF16)
    same_group_out = (iota((STATE_PER_HALF, CH_PER_HALF), 0) // SSM_STATE
                      == iota((STATE_PER_HALF, CH_PER_HALF), 1) // SSM_GROUP)
    for part, ct in enumerate((ctr_ref[...], -cti_ref[...])):
        tiled = jnp.dot(tile_out, ct.astype(BF16), preferred_element_type=F32)
        wc_ref[part * STATE_PER_HALF:(part + 1) * STATE_PER_HALF, :] = (
            jnp.where(same_group_out, tiled, 0.0).astype(BF16))


def _ssm_prep(lam_re, lam_im, log_dt, b_re, b_im, c_re, c_im):
    n_half = DEPTH * SSM_HALVES

    def per_channel(a):
        a = jnp.broadcast_to(a[:, :, None, :], (DEPTH, N_SSM_GROUPS, SSM_GROUP, SSM_STATE))
        return a.reshape(n_half, CH_PER_HALF, SSM_STATE)

    def channel_rows(a):
        return jnp.transpose(a, (0, 1, 3, 2)).reshape(n_half, CH_PER_HALF, SSM_STATE)

    def state_rows(a):
        return jnp.transpose(a.reshape(n_half, CH_PER_HALF, SSM_STATE), (0, 2, 1))

    ins = (per_channel(lam_re), per_channel(lam_im),
           per_channel(jnp.broadcast_to(log_dt[..., None], lam_re.shape)),
           channel_rows(b_re), channel_rows(b_im), state_rows(c_re), state_rows(c_im))

    def spec(a):
        return pl.BlockSpec((None,) + a.shape[1:], lambda i: (i, 0, 0))

    outs = (jax.ShapeDtypeStruct((n_half, CH_PER_HALF, SSM_STATE), F32),
            jax.ShapeDtypeStruct((n_half, CH_PER_HALF, SSM_STATE), F32),
            jax.ShapeDtypeStruct((n_half, CH_PER_HALF, RE_IM * STATE_PER_HALF), BF16),
            jax.ShapeDtypeStruct((n_half, RE_IM * STATE_PER_HALF, CH_PER_HALF), BF16))
    return pl.pallas_call(
        _ssm_prep_kernel,
        grid=(n_half,),
        in_specs=[spec(a) for a in ins],
        out_specs=tuple(spec(o) for o in outs),
        out_shape=outs,
        compiler_params=pltpu.CompilerParams(dimension_semantics=("parallel",)),
        name="ssm_prep",
    )(*ins)


def _pool_mixer(pbuf_ref, sb, time0, wpool_ref, pscale_ref):
    row = lax.broadcasted_iota(jnp.int32, (BLOCK_ROWS, POOL_GROUP), 0)
    n_pos = time0 + row // BATCH + 1
    y_pool = []
    for gi, w in enumerate(POOL_WINDOWS):
        x = pbuf_ref[sb, :, gi * POOL_GROUP:(gi + 1) * POOL_GROUP]
        s = x
        span = 1
        while span < w:
            s = s + pltpu.roll(s, span * BATCH, 0)
            span *= 2
        mean = s[POOL_HIST_ROWS:, :] / jnp.minimum(n_pos, w).astype(F32)
        d = mean - x[POOL_HIST_ROWS:, :]
        y_pool.append(jnp.dot(d.astype(BF16), wpool_ref[gi * POOL_GROUP:(gi + 1) * POOL_GROUP, :],
                              preferred_element_type=F32))
    return jnp.concatenate(y_pool, axis=1) * pscale_ref[...]


def _swap_time_pairs(x):
    n = x.shape[0]
    first_half = lax.broadcasted_iota(jnp.int32, x.shape, 0) % V7X_SUBLANES < BATCH
    return jnp.where(first_half, pltpu.roll(x, n - BATCH, 0), pltpu.roll(x, BATCH, 0))


def _scan_chain(bu_ref, sb, j, lanes, w, a_r, a_i, lo):
    im_lanes = slice(STATE_PER_HALF + lanes.start, STATE_PER_HALF + lanes.stop)
    for m in range(BLOCK_ROWS // V7X_SUBLANES):
        rows = slice(m * V7X_SUBLANES, (m + 1) * V7X_SUBLANES)
        re = bu_ref[sb, j, rows, lanes]
        im = bu_ref[sb, j, rows, im_lanes]
        x0 = jnp.where(lo, re, im)
        x1 = jnp.where(lo, im, re)
        v0 = a_r * pltpu.roll(w, BATCH, 0) + a_i * w + x0
        w = a_r * pltpu.roll(v0, BATCH, 0) - a_i * v0 + x1
        bu_ref[sb, j, rows, lanes] = jnp.where(lo, v0, w)
        bu_ref[sb, j, rows, im_lanes] = jnp.where(lo, w, v0)
    return w


def _ffn_block(h, gffn_ref, wg_ref, wu_ref, wd_ref):
    hn = _rmsnorm(h, gffn_ref[...]).astype(BF16)
    out = h
    for lo_col, hi_col in zip(FF_CHUNK_EDGES[:-1], FF_CHUNK_EDGES[1:]):
        g = jnp.dot(hn, wg_ref[:, lo_col:hi_col], preferred_element_type=F32)
        u = jnp.dot(hn, wu_ref[:, lo_col:hi_col], preferred_element_type=F32)
        a = (g * _sigmoid(g) * u).astype(BF16)
        out = out + jnp.dot(a, wd_ref[lo_col:hi_col, :], preferred_element_type=F32)
    return out


def _layer_kernel(*refs, first, final):
    refs = list(refs)
    h_ref = refs.pop(0)
    (gmix_ref, pscale_ref, wb_ref, ar_ref, ai_ref, wc_ref, dskip_ref, bglu_ref,
     gffn_ref) = [refs.pop(0) for _ in range(N_SMALL_PARAMS)]
    win_ref, wpool_ref, wglu_ref, wout_ref, wg_ref, wu_ref, wd_ref = [
        refs.pop(0) for _ in BIG_WEIGHTS]
    gfin_ref = refs.pop(0)
    next_f32 = [] if final else [refs.pop(0) for _ in BIG_WEIGHTS]
    o_ref = refs.pop(0)
    next_bf16 = [] if final else [refs.pop(0) for _ in BIG_WEIGHTS]
    hist_ref, state_ref, pbuf_ref, bu_ref = [refs.pop(0) for _ in range(4)]
    reorder_ref = refs
    step = pl.program_id(0)

    for src, dst in zip(next_f32, next_bf16):
        dst[...] = src[...].astype(BF16)

    @pl.when(step == 0)
    def _():
        hist_ref[...] = jnp.zeros_like(hist_ref)
        state_ref[...] = jnp.zeros_like(state_ref)

    lo = lax.broadcasted_iota(jnp.int32, (V7X_SUBLANES, SCAN_LANES), 0) < BATCH
    v = [[state_ref[j, :, c * SCAN_LANES:(c + 1) * SCAN_LANES]
          for c in range(STATE_PER_HALF // SCAN_LANES)] for j in range(SSM_HALVES)]
    hist = hist_ref[...]
    heads = []
    for sb in range(SUB_BLOCKS):
        rows = slice(sb * BLOCK_ROWS, (sb + 1) * BLOCK_ROWS)
        if first:
            hbuf_ref = reorder_ref[0]
            t0 = sb * TIME_BLOCK
            for k in range(LANE_SLABS):
                for b in range(BATCH):
                    hbuf_ref[k, pl.ds(b, TIME_BLOCK, stride=BATCH), :] = (
                        h_ref[b, t0:t0 + TIME_BLOCK, k * V7X_LANES:(k + 1) * V7X_LANES])
            h = jnp.concatenate([hbuf_ref[k] for k in range(LANE_SLABS)], axis=1)
        else:
            h = h_ref[rows, :]
        hn = _rmsnorm(h, gmix_ref[...])
        u = jnp.dot(hn.astype(BF16), win_ref[...], preferred_element_type=F32)

        us = u[:, D_POOL:]
        usb = us.astype(BF16)
        usb_swapped = _swap_time_pairs(us).astype(BF16)
        for j in range(SSM_HALVES):
            cols = slice(j * CH_PER_HALF, (j + 1) * CH_PER_HALF)
            bu_ref[sb, j, :, 0:STATE_PER_HALF] = jnp.dot(
                usb[:, cols], wb_ref[j, :, 0:STATE_PER_HALF], preferred_element_type=F32)
            bu_ref[sb, j, :, STATE_PER_HALF:] = jnp.dot(
                usb_swapped[:, cols], wb_ref[j, :, STATE_PER_HALF:], preferred_element_type=F32)

        up = u[:, :D_POOL]
        pbuf_ref[sb, 0:POOL_HIST_ROWS, :] = hist
        pbuf_ref[sb, POOL_HIST_ROWS:, :] = up
        hist = up[BLOCK_ROWS - POOL_HIST_ROWS:, :]
        y_pool = _pool_mixer(pbuf_ref, sb, step * STEP_TIME + sb * TIME_BLOCK, wpool_ref, pscale_ref)
        heads.append((h, us, y_pool))

    never = step < 0
    prev = None
    for sb in range(SUB_BLOCKS):
        for j in range(SSM_HALVES):
            for c in range(STATE_PER_HALF // SCAN_LANES):
                lanes = slice(c * SCAN_LANES, (c + 1) * SCAN_LANES)
                w = v[j][c]
                if prev is not None:
                    w = jnp.where(never, prev, w)
                w = _scan_chain(bu_ref, sb, j, lanes, w, ar_ref[j, :, lanes], ai_ref[j, :, lanes],
                                lo)
                v[j][c] = w
                prev = w

    for sb in range(SUB_BLOCKS):
        rows = slice(sb * BLOCK_ROWS, (sb + 1) * BLOCK_ROWS)
        h, us, y_pool = heads[sb]
        y_halves = []
        for j in range(SSM_HALVES):
            y_re = jnp.dot(bu_ref[sb, j, :, 0:STATE_PER_HALF].astype(BF16),
                           wc_ref[j, 0:STATE_PER_HALF, :], preferred_element_type=F32)
            y_im = jnp.dot(bu_ref[sb, j, :, STATE_PER_HALF:].astype(BF16),
                           wc_ref[j, STATE_PER_HALF:, :], preferred_element_type=F32)
            y_halves.append(y_re + _swap_time_pairs(y_im))
        y = jnp.concatenate(y_halves, axis=1) + dskip_ref[...] * us
        y = 0.5 * y * (1.0 + jnp.tanh(math.sqrt(2.0 / math.pi) * (y + 0.044715 * (y * y * y))))
        z = jnp.dot(y.astype(BF16), wglu_ref[...], preferred_element_type=F32) + bglu_ref[...]
        y_ssm = y * _sigmoid(z)

        cat = jnp.concatenate([y_pool, y_ssm], axis=1).astype(BF16)
        h = h + jnp.dot(cat, wout_ref[...], preferred_element_type=F32)

        out = _ffn_block(h, gffn_ref, wg_ref, wu_ref, wd_ref)
        if final:
            obuf_ref = reorder_ref[-1]
            out = _rmsnorm(out, gfin_ref[...])
            t0 = sb * TIME_BLOCK
            for k in range(LANE_SLABS):
                obuf_ref[k] = out[:, k * V7X_LANES:(k + 1) * V7X_LANES]
                for b in range(BATCH):
                    o_ref[b, t0:t0 + TIME_BLOCK, k * V7X_LANES:(k + 1) * V7X_LANES] = (
                        obuf_ref[k, pl.ds(b, TIME_BLOCK, stride=BATCH), :])
        else:
            o_ref[rows, :] = out
    hist_ref[...] = hist
    for j in range(SSM_HALVES):
        for c in range(STATE_PER_HALF // SCAN_LANES):
            state_ref[j, :, c * SCAN_LANES:(c + 1) * SCAN_LANES] = v[j][c]


def _layer_spec(a, layer):
    tail = (0,) * (a.ndim - 1)
    return pl.BlockSpec((None,) + a.shape[1:], lambda i: (layer,) + tail,
                        pipeline_mode=pl.Buffered(1))


def _resident_spec(a):
    zeros = (0,) * a.ndim
    return pl.BlockSpec(a.shape, lambda i: zeros, pipeline_mode=pl.Buffered(1))


def _cast_specs(w, layer):
    rows, cols = w.shape[1:]
    block_rows = CAST_BLOCK_ROWS if rows % N_STEPS or rows // N_STEPS % 16 else rows // N_STEPS
    last = rows // block_rows - 1
    assert rows % block_rows == 0 and last < N_STEPS
    src = pl.BlockSpec((None, block_rows, cols), lambda i: (layer, jnp.minimum(i, last), 0))
    dst = pl.BlockSpec((block_rows, cols), lambda i: (jnp.minimum(i, last), 0))
    return src, dst, jax.ShapeDtypeStruct((rows, cols), BF16)


def _layer(h, layer, small, weights, gfin, weights_f32):
    first = layer == 0
    final = layer == DEPTH - 1
    row_spec = pl.BlockSpec((STEP_ROWS, D_MODEL), lambda i: (i, 0))
    btd_spec = pl.BlockSpec((BATCH, STEP_TIME, D_MODEL), lambda i: (0, i, 0))
    scratch = [
        pltpu.VMEM((POOL_HIST_ROWS, D_POOL), F32),
        pltpu.VMEM((SSM_HALVES, V7X_SUBLANES, STATE_PER_HALF), F32),
        pltpu.VMEM((SUB_BLOCKS, BLOCK_ROWS + POOL_HIST_ROWS, D_POOL), F32),
        pltpu.VMEM((SUB_BLOCKS, SSM_HALVES, BLOCK_ROWS, RE_IM * STATE_PER_HALF), F32),
    ]
    if first:
        scratch.append(pltpu.VMEM((LANE_SLABS, BLOCK_ROWS, V7X_LANES), F32))
    if final:
        scratch.append(pltpu.VMEM((LANE_SLABS, BLOCK_ROWS, V7X_LANES), F32))
    in_specs = ([btd_spec if first else row_spec] + [_layer_spec(c, layer) for c in small]
                + [_resident_spec(w) for w in weights] + [_resident_spec(gfin)])
    operands = [h, *small, *weights, gfin]
    if final:
        out_specs = [btd_spec]
        out_shape = [jax.ShapeDtypeStruct((BATCH, SEQ, D_MODEL), F32)]
    else:
        out_specs = [row_spec]
        out_shape = [jax.ShapeDtypeStruct((SEQ * BATCH, D_MODEL), F32)]
        for w in weights_f32:
            src, dst, shape = _cast_specs(w, layer + 1)
            in_specs.append(src)
            operands.append(w)
            out_specs.append(dst)
            out_shape.append(shape)
    outs = pl.pallas_call(
        functools.partial(_layer_kernel, first=first, final=final),
        grid=(N_STEPS,),
        in_specs=in_specs,
        out_specs=out_specs,
        out_shape=out_shape,
        scratch_shapes=scratch,
        compiler_params=pltpu.CompilerParams(
            dimension_semantics=("arbitrary",), vmem_limit_bytes=V7X_VMEM_LIMIT_BYTES),
        name="layer_%d" % layer,
    )(*operands)
    return outs[0], outs[1:]


def kernel(x, norm_mix, w_in, w_pool, pool_scale, lam_re, lam_im, log_dt, b_re, b_im, c_re, c_im,
           d_skip, w_glu, b_glu, w_out, norm_ffn, w_gate, w_up, w_down, norm_final):
    assert x.shape == (BATCH, SEQ, D_MODEL) and x.dtype == F32

    abar_r, abar_i, wb, wc = _ssm_prep(lam_re, lam_im, log_dt, b_re, b_im, c_re, c_im)
    per_row = (DEPTH, SSM_HALVES, GROUPS_PER_HALF, SSM_GROUP, SSM_STATE)
    abar_r = abar_r.reshape(per_row)[:, :, :, 0, :].reshape(DEPTH, SSM_HALVES, 1, STATE_PER_HALF)
    abar_i = abar_i.reshape(per_row)[:, :, :, 0, :].reshape(DEPTH, SSM_HALVES, 1, STATE_PER_HALF)
    a_r = jnp.broadcast_to(abar_r, (DEPTH, SSM_HALVES, V7X_SUBLANES, STATE_PER_HALF))
    sign = jnp.where(jnp.arange(V7X_SUBLANES) < BATCH, -1.0, 1.0).astype(F32)[None, None, :, None]
    a_i = sign * abar_i
    wb = wb.reshape(DEPTH, SSM_HALVES, CH_PER_HALF, RE_IM * STATE_PER_HALF)
    wc = wc.reshape(DEPTH, SSM_HALVES, RE_IM * STATE_PER_HALF, CH_PER_HALF)

    small = (
        norm_mix.reshape(DEPTH, 1, D_MODEL), pool_scale.reshape(DEPTH, 1, D_POOL), wb, a_r, a_i, wc,
        d_skip.reshape(DEPTH, 1, D_SSM), b_glu.reshape(DEPTH, 1, D_SSM),
        norm_ffn.reshape(DEPTH, 1, D_MODEL))
    assert len(small) == N_SMALL_PARAMS
    weights_f32 = (w_in, w_pool.reshape(DEPTH, D_POOL, POOL_GROUP), w_glu, w_out, w_gate, w_up,
                   w_down)
    weights = [w[0].astype(BF16) for w in weights_f32]
    gfin = norm_final.reshape(1, D_MODEL)

    h = x
    for layer in range(DEPTH):
        h, weights = _layer(h, layer, small, weights, gfin, weights_f32)
    return h
```

```python
import functools
import math

import jax
import jax.numpy as jnp
from jax import lax
from jax.experimental import pallas as pl
from jax.experimental.pallas import tpu as pltpu

F32 = jnp.float32
BF16 = jnp.bfloat16

D_MODEL = 1024
BATCH = 4
SEQ = 4096
DEPTH = 4
D_POOL = 512
D_SSM = 512
POOL_WINDOWS = (2, 4, 8, 16)
POOL_GROUP = 128
SSM_GROUP = 16
N_SSM_GROUPS = 32
SSM_STATE = 64
D_FF = 2816
RMS_EPS = 1e-6

V7X_SUBLANES = 8
V7X_LANES = 128
V7X_MXU_DIM = 256
V7X_VMEM_LIMIT_BYTES = 60 * 1024 * 1024

SSM_HALVES = D_SSM // V7X_MXU_DIM
GROUPS_PER_HALF = N_SSM_GROUPS // SSM_HALVES
CH_PER_HALF = GROUPS_PER_HALF * SSM_GROUP
STATE_PER_HALF = GROUPS_PER_HALF * SSM_STATE
RE_IM = 2

TIME_BLOCK = 64
BLOCK_ROWS = TIME_BLOCK * BATCH
SUB_BLOCKS = 2
STEP_TIME = SUB_BLOCKS * TIME_BLOCK
STEP_ROWS = SUB_BLOCKS * BLOCK_ROWS
N_STEPS = SEQ // STEP_TIME
POOL_HIST_STEPS = max(POOL_WINDOWS)
POOL_HIST_ROWS = POOL_HIST_STEPS * BATCH
LANE_SLABS = D_MODEL // V7X_LANES
FF_CHUNK_EDGES = (0, 6 * V7X_MXU_DIM, D_FF)
BIG_WEIGHTS = ("w_in", "w_pool", "w_glu", "w_out", "w_gate", "w_up", "w_down")
N_SMALL_PARAMS = 9
CAST_BLOCK_ROWS = 128

assert BATCH * RE_IM == V7X_SUBLANES, "the scan tile layout needs batch*2 == 8 sublanes"
assert POOL_HIST_ROWS <= BLOCK_ROWS and D_FF % V7X_MXU_DIM == 0


def _rmsnorm(x, g):
    return x * lax.rsqrt(jnp.mean(x * x, axis=-1, keepdims=True) + RMS_EPS) * g


def _sigmoid(x):
    return 0.5 + 0.5 * jnp.tanh(0.5 * x)


def _ssm_prep_kernel(lr_ref, li_ref, ldt_ref, br_ref, bi_ref, ctr_ref, cti_ref,
                     ar_ref, ai_ref, wb_ref, wc_ref):
    lr = lr_ref[...]
    li = li_ref[...]
    dt = jnp.exp(ldt_ref[...])
    mag = jnp.exp(lr * dt)
    abar_r = mag * jnp.cos(li * dt)
    abar_i = mag * jnp.sin(li * dt)
    den = lr * lr + li * li
    nr = abar_r - 1.0
    ni = abar_i
    coef_r = (nr * lr + ni * li) / den
    coef_i = (ni * lr - nr * li) / den
    br = br_ref[...]
    bi = bi_ref[...]
    ar_ref[...] = abar_r
    ai_ref[...] = abar_i
    bbar_r = coef_r * br - coef_i * bi
    bbar_i = coef_r * bi + coef_i * br

    def iota(shape, axis):
        return lax.broadcasted_iota(jnp.int32, shape, axis)

    tile_in = (iota((SSM_STATE, STATE_PER_HALF), 1) % SSM_STATE
               == iota((SSM_STATE, STATE_PER_HALF), 0)).astype(BF16)
    same_group_in = (iota((CH_PER_HALF, STATE_PER_HALF), 0) // SSM_GROUP
                     == iota((CH_PER_HALF, STATE_PER_HALF), 1) // SSM_STATE)
    for part, bbar in enumerate((bbar_r, bbar_i)):
        tiled = jnp.dot(bbar.astype(BF16), tile_in, preferred_element_type=F32)
        wb_ref[:, part * STATE_PER_HALF:(part + 1) * STATE_PER_HALF] = (
            jnp.where(same_group_in, tiled, 0.0).astype(BF16))

    tile_out = (iota((STATE_PER_HALF, SSM_STATE), 0) % SSM_STATE
                == iota((STATE_PER_HALF, SSM_STATE), 1)).astype(BF16)
    same_group_out = (iota((STATE_PER_HALF, CH_PER_HALF), 0) // SSM_STATE
                      == iota((STATE_PER_HALF, CH_PER_HALF), 1) // SSM_GROUP)
    for part, ct in enumerate((ctr_ref[...], -cti_ref[...])):
        tiled = jnp.dot(tile_out, ct.astype(BF16), preferred_element_type=F32)
        wc_ref[part * STATE_PER_HALF:(part + 1) * STATE_PER_HALF, :] = (
            jnp.where(same_group_out, tiled, 0.0).astype(BF16))


def _ssm_prep(lam_re, lam_im, log_dt, b_re, b_im, c_re, c_im):
    n_half = DEPTH * SSM_HALVES

    def per_channel(a):
        a = jnp.broadcast_to(a[:, :, None, :], (DEPTH, N_SSM_GROUPS, SSM_GROUP, SSM_STATE))
        return a.reshape(n_half, CH_PER_HALF, SSM_STATE)

    def channel_rows(a):
        return jnp.transpose(a, (0, 1, 3, 2)).reshape(n_half, CH_PER_HALF, SSM_STATE)

    def state_rows(a):
        return jnp.transpose(a.reshape(n_half, CH_PER_HALF, SSM_STATE), (0, 2, 1))

    ins = (per_channel(lam_re), per_channel(lam_im),
           per_channel(jnp.broadcast_to(log_dt[..., None], lam_re.shape)),
           channel_rows(b_re), channel_rows(b_im), state_rows(c_re), state_rows(c_im))

    def spec(a):
        return pl.BlockSpec((None,) + a.shape[1:], lambda i: (i, 0, 0))

    outs = (jax.ShapeDtypeStruct((n_half, CH_PER_HALF, SSM_STATE), F32),
            jax.ShapeDtypeStruct((n_half, CH_PER_HALF, SSM_STATE), F32),
            jax.ShapeDtypeStruct((n_half, CH_PER_HALF, RE_IM * STATE_PER_HALF), BF16),
            jax.ShapeDtypeStruct((n_half, RE_IM * STATE_PER_HALF, CH_PER_HALF), BF16))
    return pl.pallas_call(
        _ssm_prep_kernel,
        grid=(n_half,),
        in_specs=[spec(a) for a in ins],
        out_specs=tuple(spec(o) for o in outs),
        out_shape=outs,
        compiler_params=pltpu.CompilerParams(dimension_semantics=("parallel",)),
        name="ssm_prep",
    )(*ins)


def _pool_mixer(pbuf_ref, sb, time0, wpool_ref, pscale_ref):
    row = lax.broadcasted_iota(jnp.int32, (BLOCK_ROWS, POOL_GROUP), 0)
    n_pos = time0 + row // BATCH + 1
    y_pool = []
    for gi, w in enumerate(POOL_WINDOWS):
        x = pbuf_ref[sb, :, gi * POOL_GROUP:(gi + 1) * POOL_GROUP]
        s = x
        span = 1
        while span < w:
            s = s + pltpu.roll(s, span * BATCH, 0)
            span *= 2
        mean = s[POOL_HIST_ROWS:, :] / jnp.minimum(n_pos, w).astype(F32)
        d = mean - x[POOL_HIST_ROWS:, :]
        y_pool.append(jnp.dot(d.astype(BF16), wpool_ref[gi * POOL_GROUP:(gi + 1) * POOL_GROUP, :],
                              preferred_element_type=F32))
    return jnp.concatenate(y_pool, axis=1) * pscale_ref[...]


def _swap_time_pairs(x):
    n = x.shape[0]
    first_half = lax.broadcasted_iota(jnp.int32, x.shape, 0) % V7X_SUBLANES < BATCH
    return jnp.where(first_half, pltpu.roll(x, n - BATCH, 0), pltpu.roll(x, BATCH, 0))


def _scan_block(bu_ref, sb, j, w, a_r, a_i, lo):
    for m in range(BLOCK_ROWS // V7X_SUBLANES):
        rows = slice(m * V7X_SUBLANES, (m + 1) * V7X_SUBLANES)
        re = bu_ref[sb, j, rows, 0:STATE_PER_HALF]
        im = bu_ref[sb, j, rows, STATE_PER_HALF:]
        x0 = jnp.where(lo, re, im)
        x1 = jnp.where(lo, im, re)
        v0 = a_r * pltpu.roll(w, BATCH, 0) + a_i * w + x0
        w = a_r * pltpu.roll(v0, BATCH, 0) - a_i * v0 + x1
        bu_ref[sb, j, rows, 0:STATE_PER_HALF] = jnp.where(lo, v0, w)
        bu_ref[sb, j, rows, STATE_PER_HALF:] = jnp.where(lo, w, v0)
    return w


def _ffn_block(h, gffn_ref, wg_ref, wu_ref, wd_ref):
    hn = _rmsnorm(h, gffn_ref[...]).astype(BF16)
    out = h
    for lo_col, hi_col in zip(FF_CHUNK_EDGES[:-1], FF_CHUNK_EDGES[1:]):
        g = jnp.dot(hn, wg_ref[:, lo_col:hi_col], preferred_element_type=F32)
        u = jnp.dot(hn, wu_ref[:, lo_col:hi_col], preferred_element_type=F32)
        a = (g * _sigmoid(g) * u).astype(BF16)
        out = out + jnp.dot(a, wd_ref[lo_col:hi_col, :], preferred_element_type=F32)
    return out


def _layer_kernel(*refs, first, final):
    refs = list(refs)
    h_ref = refs.pop(0)
    (gmix_ref, pscale_ref, wb_ref, ar_ref, ai_ref, wc_ref, dskip_ref, bglu_ref,
     gffn_ref) = [refs.pop(0) for _ in range(N_SMALL_PARAMS)]
    win_ref, wpool_ref, wglu_ref, wout_ref, wg_ref, wu_ref, wd_ref = [
        refs.pop(0) for _ in BIG_WEIGHTS]
    gfin_ref = refs.pop(0)
    next_f32 = [] if final else [refs.pop(0) for _ in BIG_WEIGHTS]
    o_ref = refs.pop(0)
    next_bf16 = [] if final else [refs.pop(0) for _ in BIG_WEIGHTS]
    hist_ref, state_ref, pbuf_ref, bu_ref = [refs.pop(0) for _ in range(4)]
    reorder_ref = refs
    step = pl.program_id(0)

    for src, dst in zip(next_f32, next_bf16):
        dst[...] = src[...].astype(BF16)

    @pl.when(step == 0)
    def _():
        hist_ref[...] = jnp.zeros_like(hist_ref)
        state_ref[...] = jnp.zeros_like(state_ref)

    lo = lax.broadcasted_iota(jnp.int32, (V7X_SUBLANES, STATE_PER_HALF), 0) < BATCH
    v = [state_ref[j] for j in range(SSM_HALVES)]
    hist = hist_ref[...]
    heads = []
    for sb in range(SUB_BLOCKS):
        rows = slice(sb * BLOCK_ROWS, (sb + 1) * BLOCK_ROWS)
        if first:
            hbuf_ref = reorder_ref[0]
            t0 = sb * TIME_BLOCK
            for k in range(LANE_SLABS):
                for b in range(BATCH):
                    hbuf_ref[k, pl.ds(b, TIME_BLOCK, stride=BATCH), :] = (
                        h_ref[b, t0:t0 + TIME_BLOCK, k * V7X_LANES:(k + 1) * V7X_LANES])
            h = jnp.concatenate([hbuf_ref[k] for k in range(LANE_SLABS)], axis=1)
        else:
            h = h_ref[rows, :]
        hn = _rmsnorm(h, gmix_ref[...])
        u = jnp.dot(hn.astype(BF16), win_ref[...], preferred_element_type=F32)

        us = u[:, D_POOL:]
        usb = us.astype(BF16)
        usb_swapped = _swap_time_pairs(us).astype(BF16)
        for j in range(SSM_HALVES):
            cols = slice(j * CH_PER_HALF, (j + 1) * CH_PER_HALF)
            bu_ref[sb, j, :, 0:STATE_PER_HALF] = jnp.dot(
                usb[:, cols], wb_ref[j, :, 0:STATE_PER_HALF], preferred_element_type=F32)
            bu_ref[sb, j, :, STATE_PER_HALF:] = jnp.dot(
                usb_swapped[:, cols], wb_ref[j, :, STATE_PER_HALF:], preferred_element_type=F32)

        up = u[:, :D_POOL]
        pbuf_ref[sb, 0:POOL_HIST_ROWS, :] = hist
        pbuf_ref[sb, POOL_HIST_ROWS:, :] = up
        hist = up[BLOCK_ROWS - POOL_HIST_ROWS:, :]
        y_pool = _pool_mixer(pbuf_ref, sb, step * STEP_TIME + sb * TIME_BLOCK, wpool_ref, pscale_ref)
        heads.append((h, us, y_pool))

    for sb in range(SUB_BLOCKS):
        for j in range(SSM_HALVES):
            v[j] = _scan_block(bu_ref, sb, j, v[j], ar_ref[j], ai_ref[j], lo)

    for sb in range(SUB_BLOCKS):
        rows = slice(sb * BLOCK_ROWS, (sb + 1) * BLOCK_ROWS)
        h, us, y_pool = heads[sb]
        y_halves = []
        for j in range(SSM_HALVES):
            y_re = jnp.dot(bu_ref[sb, j, :, 0:STATE_PER_HALF].astype(BF16),
                           wc_ref[j, 0:STATE_PER_HALF, :], preferred_element_type=F32)
            y_im = jnp.dot(bu_ref[sb, j, :, STATE_PER_HALF:].astype(BF16),
                           wc_ref[j, STATE_PER_HALF:, :], preferred_element_type=F32)
            y_halves.append(y_re + _swap_time_pairs(y_im))
        y = jnp.concatenate(y_halves, axis=1) + dskip_ref[...] * us
        y = 0.5 * y * (1.0 + jnp.tanh(math.sqrt(2.0 / math.pi) * (y + 0.044715 * (y * y * y))))
        z = jnp.dot(y.astype(BF16), wglu_ref[...], preferred_element_type=F32) + bglu_ref[...]
        y_ssm = y * _sigmoid(z)

        cat = jnp.concatenate([y_pool, y_ssm], axis=1).astype(BF16)
        h = h + jnp.dot(cat, wout_ref[...], preferred_element_type=F32)

        out = _ffn_block(h, gffn_ref, wg_ref, wu_ref, wd_ref)
        if final:
            obuf_ref = reorder_ref[-1]
            out = _rmsnorm(out, gfin_ref[...])
            t0 = sb * TIME_BLOCK
            for k in range(LANE_SLABS):
                obuf_ref[k] = out[:, k * V7X_LANES:(k + 1) * V7X_LANES]
                for b in range(BATCH):
                    o_ref[b, t0:t0 + TIME_BLOCK, k * V7X_LANES:(k + 1) * V7X_LANES] = (
                        obuf_ref[k, pl.ds(b, TIME_BLOCK, stride=BATCH), :])
        else:
            o_ref[rows, :] = out
    hist_ref[...] = hist
    for j in range(SSM_HALVES):
        state_ref[j] = v[j]


def _layer_spec(a, layer):
    tail = (0,) * (a.ndim - 1)
    return pl.BlockSpec((None,) + a.shape[1:], lambda i: (layer,) + tail,
                        pipeline_mode=pl.Buffered(1))


def _resident_spec(a):
    zeros = (0,) * a.ndim
    return pl.BlockSpec(a.shape, lambda i: zeros, pipeline_mode=pl.Buffered(1))


def _cast_specs(w, layer):
    rows, cols = w.shape[1:]
    block_rows = CAST_BLOCK_ROWS if rows % N_STEPS or rows // N_STEPS % 16 else rows // N_STEPS
    last = rows // block_rows - 1
    assert rows % block_rows == 0 and last < N_STEPS
    src = pl.BlockSpec((None, block_rows, cols), lambda i: (layer, jnp.minimum(i, last), 0))
    dst = pl.BlockSpec((block_rows, cols), lambda i: (jnp.minimum(i, last), 0))
    return src, dst, jax.ShapeDtypeStruct((rows, cols), BF16)


def _layer(h, layer, small, weights, gfin, weights_f32):
    first = layer == 0
    final = layer == DEPTH - 1
    row_spec = pl.BlockSpec((STEP_ROWS, D_MODEL), lambda i: (i, 0))
    btd_spec = pl.BlockSpec((BATCH, STEP_TIME, D_MODEL), lambda i: (0, i, 0))
    scratch = [
        pltpu.VMEM((POOL_HIST_ROWS, D_POOL), F32),
        pltpu.VMEM((SSM_HALVES, V7X_SUBLANES, STATE_PER_HALF), F32),
        pltpu.VMEM((SUB_BLOCKS, BLOCK_ROWS + POOL_HIST_ROWS, D_POOL), F32),
        pltpu.VMEM((SUB_BLOCKS, SSM_HALVES, BLOCK_ROWS, RE_IM * STATE_PER_HALF), F32),
    ]
    if first:
        scratch.append(pltpu.VMEM((LANE_SLABS, BLOCK_ROWS, V7X_LANES), F32))
    if final:
        scratch.append(pltpu.VMEM((LANE_SLABS, BLOCK_ROWS, V7X_LANES), F32))
    in_specs = ([btd_spec if first else row_spec] + [_layer_spec(c, layer) for c in small]
                + [_resident_spec(w) for w in weights] + [_resident_spec(gfin)])
    operands = [h, *small, *weights, gfin]
    if final:
        out_specs = [btd_spec]
        out_shape = [jax.ShapeDtypeStruct((BATCH, SEQ, D_MODEL), F32)]
    else:
        out_specs = [row_spec]
        out_shape = [jax.ShapeDtypeStruct((SEQ * BATCH, D_MODEL), F32)]
        for w in weights_f32:
            src, dst, shape = _cast_specs(w, layer + 1)
            in_specs.append(src)
            operands.append(w)
            out_specs.append(dst)
            out_shape.append(shape)
    outs = pl.pallas_call(
        functools.partial(_layer_kernel, first=first, final=final),
        grid=(N_STEPS,),
        in_specs=in_specs,
        out_specs=out_specs,
        out_shape=out_shape,
        scratch_shapes=scratch,
        compiler_params=pltpu.CompilerParams(
            dimension_semantics=("arbitrary",), vmem_limit_bytes=V7X_VMEM_LIMIT_BYTES),
        name="layer_%d" % layer,
    )(*operands)
    return outs[0], outs[1:]


def kernel(x, norm_mix, w_in, w_pool, pool_scale, lam_re, lam_im, log_dt, b_re, b_im, c_re, c_im,
           d_skip, w_glu, b_glu, w_out, norm_ffn, w_gate, w_up, w_down, norm_final):
    assert x.shape == (BATCH, SEQ, D_MODEL) and x.dtype == F32

    abar_r, abar_i, wb, wc = _ssm_prep(lam_re, lam_im, log_dt, b_re, b_im, c_re, c_im)
    per_row = (DEPTH, SSM_HALVES, GROUPS_PER_HALF, SSM_GROUP, SSM_STATE)
    abar_r = abar_r.reshape(per_row)[:, :, :, 0, :].reshape(DEPTH, SSM_HALVES, 1, STATE_PER_HALF)
    abar_i = abar_i.reshape(per_row)[:, :, :, 0, :].reshape(DEPTH, SSM_HALVES, 1, STATE_PER_HALF)
    a_r = jnp.broadcast_to(abar_r, (DEPTH, SSM_HALVES, V7X_SUBLANES, STATE_PER_HALF))
    sign = jnp.where(jnp.arange(V7X_SUBLANES) < BATCH, -1.0, 1.0).astype(F32)[None, None, :, None]
    a_i = sign * abar_i
    wb = wb.reshape(DEPTH, SSM_HALVES, CH_PER_HALF, RE_IM * STATE_PER_HALF)
    wc = wc.reshape(DEPTH, SSM_HALVES, RE_IM * STATE_PER_HALF, CH_PER_HALF)

    small = (
        norm_mix.reshape(DEPTH, 1, D_MODEL), pool_scale.reshape(DEPTH, 1, D_POOL), wb, a_r, a_i, wc,
        d_skip.reshape(DEPTH, 1, D_SSM), b_glu.reshape(DEPTH, 1, D_SSM),
        norm_ffn.reshape(DEPTH, 1, D_MODEL))
    assert len(small) == N_SMALL_PARAMS
    weights_f32 = (w_in, w_pool.reshape(DEPTH, D_POOL, POOL_GROUP), w_glu, w_out, w_gate, w_up,
                   w_down)
    weights = [w[0].astype(BF16) for w in weights_f32]
    gfin = norm_final.reshape(1, D_MODEL)

    h = x
    for layer in range(DEPTH):
        h, weights = _layer(h, layer, small, weights, gfin, weights_f32)
    return h
```

```python
import functools
import math

import jax
import jax.numpy as jnp
from jax import lax
from jax.experimental import pallas as pl
from jax.experimental.pallas import tpu as pltpu

F32 = jnp.float32
BF16 = jnp.bfloat16

D_MODEL = 1024
BATCH = 4
SEQ = 4096
DEPTH = 4
D_POOL = 512
D_SSM = 512
POOL_WINDOWS = (2, 4, 8, 16)
POOL_GROUP = 128
SSM_GROUP = 16
N_SSM_GROUPS = 32
SSM_STATE = 64
D_FF = 2816
RMS_EPS = 1e-6

V7X_SUBLANES = 8
V7X_LANES = 128
V7X_MXU_DIM = 256
V7X_VMEM_LIMIT_BYTES = 60 * 1024 * 1024

SSM_HALVES = D_SSM // V7X_MXU_DIM
GROUPS_PER_HALF = N_SSM_GROUPS // SSM_HALVES
CH_PER_HALF = GROUPS_PER_HALF * SSM_GROUP
STATE_PER_HALF = GROUPS_PER_HALF * SSM_STATE
RE_IM = 2

TIME_BLOCK = 64
BLOCK_ROWS = TIME_BLOCK * BATCH
SUB_BLOCKS = 2
STEP_TIME = SUB_BLOCKS * TIME_BLOCK
STEP_ROWS = SUB_BLOCKS * BLOCK_ROWS
N_STEPS = SEQ // STEP_TIME
POOL_HIST_STEPS = max(POOL_WINDOWS)
POOL_HIST_ROWS = POOL_HIST_STEPS * BATCH
LANE_SLABS = D_MODEL // V7X_LANES
FF_CHUNK_EDGES = (0, 6 * V7X_MXU_DIM, D_FF)
BIG_WEIGHTS = ("w_in", "w_pool", "w_glu", "w_out", "w_gate", "w_up", "w_down")
N_SMALL_PARAMS = 9
CAST_BLOCK_ROWS = 128

assert BATCH * RE_IM == V7X_SUBLANES, "the scan tile layout needs batch*2 == 8 sublanes"
assert POOL_HIST_ROWS <= BLOCK_ROWS and D_FF % V7X_MXU_DIM == 0


def _rmsnorm(x, g):
    return x * lax.rsqrt(jnp.mean(x * x, axis=-1, keepdims=True) + RMS_EPS) * g


def _sigmoid(x):
    return 0.5 + 0.5 * jnp.tanh(0.5 * x)


def _ssm_prep_kernel(lr_ref, li_ref, ldt_ref, br_ref, bi_ref, ctr_ref, cti_ref, *refs):
    n_w = len(BIG_WEIGHTS)
    first_f32 = refs[:n_w]
    ar_ref, ai_ref, wb_ref, wc_ref = refs[n_w:n_w + 4]
    first_bf16 = refs[n_w + 4:]
    for src, dst in zip(first_f32, first_bf16):
        dst[...] = src[...].astype(BF16)

    lr = lr_ref[...]
    li = li_ref[...]
    dt = jnp.exp(ldt_ref[...])
    mag = jnp.exp(lr * dt)
    abar_r = mag * jnp.cos(li * dt)
    abar_i = mag * jnp.sin(li * dt)
    den = lr * lr + li * li
    nr = abar_r - 1.0
    ni = abar_i
    coef_r = (nr * lr + ni * li) / den
    coef_i = (ni * lr - nr * li) / den
    br = br_ref[...]
    bi = bi_ref[...]
    ar_ref[...] = abar_r
    ai_ref[...] = abar_i
    bbar_r = coef_r * br - coef_i * bi
    bbar_i = coef_r * bi + coef_i * br

    def iota(shape, axis):
        return lax.broadcasted_iota(jnp.int32, shape, axis)

    tile_in = (iota((SSM_STATE, STATE_PER_HALF), 1) % SSM_STATE
               == iota((SSM_STATE, STATE_PER_HALF), 0)).astype(BF16)
    same_group_in = (iota((CH_PER_HALF, STATE_PER_HALF), 0) // SSM_GROUP
                     == iota((CH_PER_HALF, STATE_PER_HALF), 1) // SSM_STATE)
    for part, bbar in enumerate((bbar_r, bbar_i)):
        tiled = jnp.dot(bbar.astype(BF16), tile_in, preferred_element_type=F32)
        wb_ref[:, part * STATE_PER_HALF:(part + 1) * STATE_PER_HALF] = (
            jnp.where(same_group_in, tiled, 0.0).astype(BF16))

    tile_out = (iota((STATE_PER_HALF, SSM_STATE), 0) % SSM_STATE
                == iota((STATE_PER_HALF, SSM_STATE), 1)).astype(BF16)
    same_group_out = (iota((STATE_PER_HALF, CH_PER_HALF), 0) // SSM_STATE
                      == iota((STATE_PER_HALF, CH_PER_HALF), 1) // SSM_GROUP)
    for part, ct in enumerate((ctr_ref[...], -cti_ref[...])):
        tiled = jnp.dot(tile_out, ct.astype(BF16), preferred_element_type=F32)
        wc_ref[part * STATE_PER_HALF:(part + 1) * STATE_PER_HALF, :] = (
            jnp.where(same_group_out, tiled, 0.0).astype(BF16))


def _ssm_prep(lam_re, lam_im, log_dt, b_re, b_im, c_re, c_im, weights_f32):
    n_half = DEPTH * SSM_HALVES

    def per_channel(a):
        a = jnp.broadcast_to(a[:, :, None, :], (DEPTH, N_SSM_GROUPS, SSM_GROUP, SSM_STATE))
        return a.reshape(n_half, CH_PER_HALF, SSM_STATE)

    def channel_rows(a):
        return jnp.transpose(a, (0, 1, 3, 2)).reshape(n_half, CH_PER_HALF, SSM_STATE)

    def state_rows(a):
        return jnp.transpose(a.reshape(n_half, CH_PER_HALF, SSM_STATE), (0, 2, 1))

    ins = (per_channel(lam_re), per_channel(lam_im),
           per_channel(jnp.broadcast_to(log_dt[..., None], lam_re.shape)),
           channel_rows(b_re), channel_rows(b_im), state_rows(c_re), state_rows(c_im))

    def spec(a):
        return pl.BlockSpec((None,) + a.shape[1:], lambda i: (i, 0, 0))

    outs = (jax.ShapeDtypeStruct((n_half, CH_PER_HALF, SSM_STATE), F32),
            jax.ShapeDtypeStruct((n_half, CH_PER_HALF, SSM_STATE), F32),
            jax.ShapeDtypeStruct((n_half, CH_PER_HALF, RE_IM * STATE_PER_HALF), BF16),
            jax.ShapeDtypeStruct((n_half, RE_IM * STATE_PER_HALF, CH_PER_HALF), BF16))
    in_specs = [spec(a) for a in ins]
    out_specs = [spec(o) for o in outs]
    out_shape = list(outs)
    for w in weights_f32:
        rows, cols = w.shape[1:]
        block_rows = rows // n_half
        assert rows % n_half == 0 and block_rows % 16 == 0
        in_specs.append(pl.BlockSpec((None, block_rows, cols), lambda i: (0, i, 0)))
        out_specs.append(pl.BlockSpec((block_rows, cols), lambda i: (i, 0)))
        out_shape.append(jax.ShapeDtypeStruct((rows, cols), BF16))
    res = pl.pallas_call(
        _ssm_prep_kernel,
        grid=(n_half,),
        in_specs=in_specs,
        out_specs=out_specs,
        out_shape=out_shape,
        compiler_params=pltpu.CompilerParams(
            dimension_semantics=("parallel",), vmem_limit_bytes=V7X_VMEM_LIMIT_BYTES),
        name="ssm_prep",
    )(*ins, *weights_f32)
    return res[:4], res[4:]


def _pool_mixer(pbuf_ref, sb, time0, wpool_ref, pscale_ref):
    row = lax.broadcasted_iota(jnp.int32, (BLOCK_ROWS, POOL_GROUP), 0)
    n_pos = time0 + row // BATCH + 1
    y_pool = []
    for gi, w in enumerate(POOL_WINDOWS):
        x = pbuf_ref[sb, :, gi * POOL_GROUP:(gi + 1) * POOL_GROUP]
        s = x
        span = 1
        while span < w:
            s = s + pltpu.roll(s, span * BATCH, 0)
            span *= 2
        mean = s[POOL_HIST_ROWS:, :] / jnp.minimum(n_pos, w).astype(F32)
        d = mean - x[POOL_HIST_ROWS:, :]
        y_pool.append(jnp.dot(d.astype(BF16), wpool_ref[gi * POOL_GROUP:(gi + 1) * POOL_GROUP, :],
                              preferred_element_type=F32))
    return jnp.concatenate(y_pool, axis=1) * pscale_ref[...]


def _swap_time_pairs(x):
    n = x.shape[0]
    first_half = lax.broadcasted_iota(jnp.int32, x.shape, 0) % V7X_SUBLANES < BATCH
    return jnp.where(first_half, pltpu.roll(x, n - BATCH, 0), pltpu.roll(x, BATCH, 0))


def _scan_block(bu_ref, sb, j, w, a_r, a_i, lo):
    for m in range(BLOCK_ROWS // V7X_SUBLANES):
        rows = slice(m * V7X_SUBLANES, (m + 1) * V7X_SUBLANES)
        re = bu_ref[sb, j, rows, 0:STATE_PER_HALF]
        im = bu_ref[sb, j, rows, STATE_PER_HALF:]
        x0 = jnp.where(lo, re, im)
        x1 = jnp.where(lo, im, re)
        v0 = a_r * pltpu.roll(w, BATCH, 0) + a_i * w + x0
        w = a_r * pltpu.roll(v0, BATCH, 0) - a_i * v0 + x1
        bu_ref[sb, j, rows, 0:STATE_PER_HALF] = jnp.where(lo, v0, w)
        bu_ref[sb, j, rows, STATE_PER_HALF:] = jnp.where(lo, w, v0)
    return w


def _ffn_block(h, gffn_ref, wg_ref, wu_ref, wd_ref):
    hn = _rmsnorm(h, gffn_ref[...]).astype(BF16)
    out = h
    for lo_col, hi_col in zip(FF_CHUNK_EDGES[:-1], FF_CHUNK_EDGES[1:]):
        g = jnp.dot(hn, wg_ref[:, lo_col:hi_col], preferred_element_type=F32)
        u = jnp.dot(hn, wu_ref[:, lo_col:hi_col], preferred_element_type=F32)
        a = (g * _sigmoid(g) * u).astype(BF16)
        out = out + jnp.dot(a, wd_ref[lo_col:hi_col, :], preferred_element_type=F32)
    return out


def _layer_kernel(*refs, first, final):
    refs = list(refs)
    h_ref = refs.pop(0)
    (gmix_ref, pscale_ref, wb_ref, ar_ref, ai_ref, wc_ref, dskip_ref, bglu_ref,
     gffn_ref) = [refs.pop(0) for _ in range(N_SMALL_PARAMS)]
    win_ref, wpool_ref, wglu_ref, wout_ref, wg_ref, wu_ref, wd_ref = [
        refs.pop(0) for _ in BIG_WEIGHTS]
    gfin_ref = refs.pop(0)
    next_f32 = [] if final else [refs.pop(0) for _ in BIG_WEIGHTS]
    o_ref = refs.pop(0)
    next_bf16 = [] if final else [refs.pop(0) for _ in BIG_WEIGHTS]
    hist_ref, state_ref, pbuf_ref, bu_ref = [refs.pop(0) for _ in range(4)]
    reorder_ref = refs
    step = pl.program_id(0)

    for src, dst in zip(next_f32, next_bf16):
        dst[...] = src[...].astype(BF16)

    @pl.when(step == 0)
    def _():
        hist_ref[...] = jnp.zeros_like(hist_ref)
        state_ref[...] = jnp.zeros_like(state_ref)

    lo = lax.broadcasted_iota(jnp.int32, (V7X_SUBLANES, STATE_PER_HALF), 0) < BATCH
    v = [state_ref[j] for j in range(SSM_HALVES)]
    hist = hist_ref[...]
    heads = []
    for sb in range(SUB_BLOCKS):
        rows = slice(sb * BLOCK_ROWS, (sb + 1) * BLOCK_ROWS)
        if first:
            hbuf_ref = reorder_ref[0]
            t0 = sb * TIME_BLOCK
            for k in range(LANE_SLABS):
                for b in range(BATCH):
                    hbuf_ref[k, pl.ds(b, TIME_BLOCK, stride=BATCH), :] = (
                        h_ref[b, t0:t0 + TIME_BLOCK, k * V7X_LANES:(k + 1) * V7X_LANES])
            h = jnp.concatenate([hbuf_ref[k] for k in range(LANE_SLABS)], axis=1)
        else:
            h = h_ref[rows, :]
        rstd = lax.rsqrt(jnp.mean(h * h, axis=-1, keepdims=True) + RMS_EPS)
        u = jnp.dot((h * gmix_ref[...]).astype(BF16), win_ref[...],
                    preferred_element_type=F32) * rstd

        us = u[:, D_POOL:]
        usb = us.astype(BF16)
        usb_swapped = _swap_time_pairs(us).astype(BF16)
        for j in range(SSM_HALVES):
            cols = slice(j * CH_PER_HALF, (j + 1) * CH_PER_HALF)
            bu_ref[sb, j, :, 0:STATE_PER_HALF] = jnp.dot(
                usb[:, cols], wb_ref[j, :, 0:STATE_PER_HALF], preferred_element_type=F32)
            bu_ref[sb, j, :, STATE_PER_HALF:] = jnp.dot(
                usb_swapped[:, cols], wb_ref[j, :, STATE_PER_HALF:], preferred_element_type=F32)

        up = u[:, :D_POOL]
        pbuf_ref[sb, 0:POOL_HIST_ROWS, :] = hist
        pbuf_ref[sb, POOL_HIST_ROWS:, :] = up
        hist = up[BLOCK_ROWS - POOL_HIST_ROWS:, :]
        y_pool = _pool_mixer(pbuf_ref, sb, step * STEP_TIME + sb * TIME_BLOCK, wpool_ref, pscale_ref)
        heads.append((h, us, y_pool))

    for sb in range(SUB_BLOCKS):
        for j in range(SSM_HALVES):
            v[j] = _scan_block(bu_ref, sb, j, v[j], ar_ref[j], ai_ref[j], lo)

    for sb in range(SUB_BLOCKS):
        rows = slice(sb * BLOCK_ROWS, (sb + 1) * BLOCK_ROWS)
        h, us, y_pool = heads[sb]
        y_halves = []
        for j in range(SSM_HALVES):
            y_re = jnp.dot(bu_ref[sb, j, :, 0:STATE_PER_HALF].astype(BF16),
                           wc_ref[j, 0:STATE_PER_HALF, :], preferred_element_type=F32)
            y_im = jnp.dot(bu_ref[sb, j, :, STATE_PER_HALF:].astype(BF16),
                           wc_ref[j, STATE_PER_HALF:, :], preferred_element_type=F32)
            y_halves.append(y_re + _swap_time_pairs(y_im))
        y = jnp.concatenate(y_halves, axis=1) + dskip_ref[...] * us
        y = 0.5 * y * (1.0 + jnp.tanh(math.sqrt(2.0 / math.pi) * (y + 0.044715 * (y * y * y))))
        z = jnp.dot(y.astype(BF16), wglu_ref[...], preferred_element_type=F32) + bglu_ref[...]
        y_ssm = y * _sigmoid(z)

        cat = jnp.concatenate([y_pool, y_ssm], axis=1).astype(BF16)
        h = h + jnp.dot(cat, wout_ref[...], preferred_element_type=F32)

        out = _ffn_block(h, gffn_ref, wg_ref, wu_ref, wd_ref)
        if final:
            obuf_ref = reorder_ref[-1]
            out = _rmsnorm(out, gfin_ref[...])
            t0 = sb * TIME_BLOCK
            for k in range(LANE_SLABS):
                obuf_ref[k] = out[:, k * V7X_LANES:(k + 1) * V7X_LANES]
                for b in range(BATCH):
                    o_ref[b, t0:t0 + TIME_BLOCK, k * V7X_LANES:(k + 1) * V7X_LANES] = (
                        obuf_ref[k, pl.ds(b, TIME_BLOCK, stride=BATCH), :])
        else:
            o_ref[rows, :] = out
    hist_ref[...] = hist
    for j in range(SSM_HALVES):
        state_ref[j] = v[j]


def _layer_spec(a, layer):
    tail = (0,) * (a.ndim - 1)
    return pl.BlockSpec((None,) + a.shape[1:], lambda i: (layer,) + tail,
                        pipeline_mode=pl.Buffered(1))


def _resident_spec(a):
    zeros = (0,) * a.ndim
    return pl.BlockSpec(a.shape, lambda i: zeros, pipeline_mode=pl.Buffered(1))


def _cast_specs(w, layer):
    rows, cols = w.shape[1:]
    block_rows = CAST_BLOCK_ROWS if rows % N_STEPS or rows // N_STEPS % 16 else rows // N_STEPS
    last = rows // block_rows - 1
    assert rows % block_rows == 0 and last < N_STEPS
    src = pl.BlockSpec((None, block_rows, cols), lambda i: (layer, jnp.minimum(i, last), 0))
    dst = pl.BlockSpec((block_rows, cols), lambda i: (jnp.minimum(i, last), 0))
    return src, dst, jax.ShapeDtypeStruct((rows, cols), BF16)


def _layer(h, layer, small, weights, gfin, weights_f32):
    first = layer == 0
    final = layer == DEPTH - 1
    row_spec = pl.BlockSpec((STEP_ROWS, D_MODEL), lambda i: (i, 0))
    btd_spec = pl.BlockSpec((BATCH, STEP_TIME, D_MODEL), lambda i: (0, i, 0))
    scratch = [
        pltpu.VMEM((POOL_HIST_ROWS, D_POOL), F32),
        pltpu.VMEM((SSM_HALVES, V7X_SUBLANES, STATE_PER_HALF), F32),
        pltpu.VMEM((SUB_BLOCKS, BLOCK_ROWS + POOL_HIST_ROWS, D_POOL), F32),
        pltpu.VMEM((SUB_BLOCKS, SSM_HALVES, BLOCK_ROWS, RE_IM * STATE_PER_HALF), F32),
    ]
    if first:
        scratch.append(pltpu.VMEM((LANE_SLABS, BLOCK_ROWS, V7X_LANES), F32))
    if final:
        scratch.append(pltpu.VMEM((LANE_SLABS, BLOCK_ROWS, V7X_LANES), F32))
    in_specs = ([btd_spec if first else row_spec] + [_layer_spec(c, layer) for c in small]
                + [_resident_spec(w) for w in weights] + [_resident_spec(gfin)])
    operands = [h, *small, *weights, gfin]
    if final:
        out_specs = [btd_spec]
        out_shape = [jax.ShapeDtypeStruct((BATCH, SEQ, D_MODEL), F32)]
    else:
        out_specs = [row_spec]
        out_shape = [jax.ShapeDtypeStruct((SEQ * BATCH, D_MODEL), F32)]
        for w in weights_f32:
            src, dst, shape = _cast_specs(w, layer + 1)
            in_specs.append(src)
            operands.append(w)
            out_specs.append(dst)
            out_shape.append(shape)
    outs = pl.pallas_call(
        functools.partial(_layer_kernel, first=first, final=final),
        grid=(N_STEPS,),
        in_specs=in_specs,
        out_specs=out_specs,
        out_shape=out_shape,
        scratch_shapes=scratch,
        compiler_params=pltpu.CompilerParams(
            dimension_semantics=("arbitrary",), vmem_limit_bytes=V7X_VMEM_LIMIT_BYTES),
        name="layer_%d" % layer,
    )(*operands)
    return outs[0], outs[1:]


def kernel(x, norm_mix, w_in, w_pool, pool_scale, lam_re, lam_im, log_dt, b_re, b_im, c_re, c_im,
           d_skip, w_glu, b_glu, w_out, norm_ffn, w_gate, w_up, w_down, norm_final):
    assert x.shape == (BATCH, SEQ, D_MODEL) and x.dtype == F32

    weights_f32 = (w_in, w_pool.reshape(DEPTH, D_POOL, POOL_GROUP), w_glu, w_out, w_gate, w_up,
                   w_down)
    (abar_r, abar_i, wb, wc), weights = _ssm_prep(lam_re, lam_im, log_dt, b_re, b_im, c_re, c_im,
                                                  weights_f32)
    per_row = (DEPTH, SSM_HALVES, GROUPS_PER_HALF, SSM_GROUP, SSM_STATE)
    abar_r = abar_r.reshape(per_row)[:, :, :, 0, :].reshape(DEPTH, SSM_HALVES, 1, STATE_PER_HALF)
    abar_i = abar_i.reshape(per_row)[:, :, :, 0, :].reshape(DEPTH, SSM_HALVES, 1, STATE_PER_HALF)
    a_r = jnp.broadcast_to(abar_r, (DEPTH, SSM_HALVES, V7X_SUBLANES, STATE_PER_HALF))
    sign = jnp.where(jnp.arange(V7X_SUBLANES) < BATCH, -1.0, 1.0).astype(F32)[None, None, :, None]
    a_i = sign * abar_i
    wb = wb.reshape(DEPTH, SSM_HALVES, CH_PER_HALF, RE_IM * STATE_PER_HALF)
    wc = wc.reshape(DEPTH, SSM_HALVES, RE_IM * STATE_PER_HALF, CH_PER_HALF)

    small = (
        norm_mix.reshape(DEPTH, 1, D_MODEL), pool_scale.reshape(DEPTH, 1, D_POOL), wb, a_r, a_i, wc,
        d_skip.reshape(DEPTH, 1, D_SSM), b_glu.reshape(DEPTH, 1, D_SSM),
        norm_ffn.reshape(DEPTH, 1, D_MODEL))
    assert len(small) == N_SMALL_PARAMS
    gfin = norm_final.reshape(1, D_MODEL)

    h = x
    for layer in range(DEPTH):
        h, weights = _layer(h, layer, small, weights, gfin, weights_f32)
    return h
```

```python
import functools
import math

import jax
import jax.numpy as jnp
from jax import lax
from jax.experimental import pallas as pl
from jax.experimental.pallas import tpu as pltpu

F32 = jnp.float32
BF16 = jnp.bfloat16

D_MODEL = 1024
BATCH = 4
SEQ = 4096
DEPTH = 4
D_POOL = 512
D_SSM = 512
POOL_WINDOWS = (2, 4, 8, 16)
POOL_GROUP = 128
SSM_GROUP = 16
N_SSM_GROUPS = 32
SSM_STATE = 64
D_FF = 2816
RMS_EPS = 1e-6

V7X_SUBLANES = 8
V7X_LANES = 128
V7X_MXU_DIM = 256
V7X_VMEM_LIMIT_BYTES = 60 * 1024 * 1024

SSM_HALVES = D_SSM // V7X_MXU_DIM
GROUPS_PER_HALF = N_SSM_GROUPS // SSM_HALVES
CH_PER_HALF = GROUPS_PER_HALF * SSM_GROUP
STATE_PER_HALF = GROUPS_PER_HALF * SSM_STATE
RE_IM = 2

TIME_BLOCK = 64
BLOCK_ROWS = TIME_BLOCK * BATCH
SUB_BLOCKS = 2
STEP_TIME = SUB_BLOCKS * TIME_BLOCK
STEP_ROWS = SUB_BLOCKS * BLOCK_ROWS
N_STEPS = SEQ // STEP_TIME
POOL_HIST_STEPS = max(POOL_WINDOWS)
POOL_HIST_ROWS = POOL_HIST_STEPS * BATCH
LANE_SLABS = D_MODEL // V7X_LANES
FF_CHUNK_EDGES = (0, 6 * V7X_MXU_DIM, D_FF)
BIG_WEIGHTS = ("w_in", "w_pool", "w_glu", "w_out", "w_gate", "w_up", "w_down")
N_SMALL_PARAMS = 9
CAST_BLOCK_ROWS = 128

assert BATCH * RE_IM == V7X_SUBLANES, "the scan tile layout needs batch*2 == 8 sublanes"
assert POOL_HIST_ROWS <= BLOCK_ROWS and D_FF % V7X_MXU_DIM == 0


def _rmsnorm(x, g):
    return x * lax.rsqrt(jnp.mean(x * x, axis=-1, keepdims=True) + RMS_EPS) * g


def _sigmoid(x):
    return 0.5 + 0.5 * jnp.tanh(0.5 * x)


def _ssm_prep_kernel(lr_ref, li_ref, ldt_ref, br_ref, bi_ref, ctr_ref, cti_ref, *refs):
    n_w = len(BIG_WEIGHTS)
    first_f32 = refs[:n_w]
    ar_ref, ai_ref, wb_ref, wc_ref = refs[n_w:n_w + 4]
    first_bf16 = refs[n_w + 4:]
    for src, dst in zip(first_f32, first_bf16):
        dst[...] = src[...].astype(BF16)

    lr = lr_ref[...]
    li = li_ref[...]
    dt = jnp.exp(ldt_ref[...])
    mag = jnp.exp(lr * dt)
    abar_r = mag * jnp.cos(li * dt)
    abar_i = mag * jnp.sin(li * dt)
    den = lr * lr + li * li
    nr = abar_r - 1.0
    ni = abar_i
    coef_r = (nr * lr + ni * li) / den
    coef_i = (ni * lr - nr * li) / den
    br = br_ref[...]
    bi = bi_ref[...]
    ar_ref[...] = abar_r
    ai_ref[...] = abar_i
    bbar_r = coef_r * br - coef_i * bi
    bbar_i = coef_r * bi + coef_i * br

    def iota(shape, axis):
        return lax.broadcasted_iota(jnp.int32, shape, axis)

    tile_in = (iota((SSM_STATE, STATE_PER_HALF), 1) % SSM_STATE
               == iota((SSM_STATE, STATE_PER_HALF), 0)).astype(BF16)
    same_group_in = (iota((CH_PER_HALF, STATE_PER_HALF), 0) // SSM_GROUP
                     == iota((CH_PER_HALF, STATE_PER_HALF), 1) // SSM_STATE)
    for part, bbar in enumerate((bbar_r, bbar_i)):
        tiled = jnp.dot(bbar.astype(BF16), tile_in, preferred_element_type=F32)
        wb_ref[:, part * STATE_PER_HALF:(part + 1) * STATE_PER_HALF] = (
            jnp.where(same_group_in, tiled, 0.0).astype(BF16))

    tile_out = (iota((STATE_PER_HALF, SSM_STATE), 0) % SSM_STATE
                == iota((STATE_PER_HALF, SSM_STATE), 1)).astype(BF16)
    same_group_out = (iota((STATE_PER_HALF, CH_PER_HALF), 0) // SSM_STATE
                      == iota((STATE_PER_HALF, CH_PER_HALF), 1) // SSM_GROUP)
    for part, ct in enumerate((ctr_ref[...], -cti_ref[...])):
        tiled = jnp.dot(tile_out, ct.astype(BF16), preferred_element_type=F32)
        wc_ref[part * STATE_PER_HALF:(part + 1) * STATE_PER_HALF, :] = (
            jnp.where(same_group_out, tiled, 0.0).astype(BF16))


def _ssm_prep(lam_re, lam_im, log_dt, b_re, b_im, c_re, c_im, weights_f32):
    n_half = DEPTH * SSM_HALVES

    def per_channel(a):
        a = jnp.broadcast_to(a[:, :, None, :], (DEPTH, N_SSM_GROUPS, SSM_GROUP, SSM_STATE))
        return a.reshape(n_half, CH_PER_HALF, SSM_STATE)

    def channel_rows(a):
        return jnp.transpose(a, (0, 1, 3, 2)).reshape(n_half, CH_PER_HALF, SSM_STATE)

    def state_rows(a):
        return jnp.transpose(a.reshape(n_half, CH_PER_HALF, SSM_STATE), (0, 2, 1))

    ins = (per_channel(lam_re), per_channel(lam_im),
           per_channel(jnp.broadcast_to(log_dt[..., None], lam_re.shape)),
           channel_rows(b_re), channel_rows(b_im), state_rows(c_re), state_rows(c_im))

    def spec(a):
        return pl.BlockSpec((None,) + a.shape[1:], lambda i: (i, 0, 0))

    outs = (jax.ShapeDtypeStruct((n_half, CH_PER_HALF, SSM_STATE), F32),
            jax.ShapeDtypeStruct((n_half, CH_PER_HALF, SSM_STATE), F32),
            jax.ShapeDtypeStruct((n_half, CH_PER_HALF, RE_IM * STATE_PER_HALF), BF16),
            jax.ShapeDtypeStruct((n_half, RE_IM * STATE_PER_HALF, CH_PER_HALF), BF16))
    in_specs = [spec(a) for a in ins]
    out_specs = [spec(o) for o in outs]
    out_shape = list(outs)
    for w in weights_f32:
        rows, cols = w.shape[1:]
        block_rows = rows // n_half
        assert rows % n_half == 0 and block_rows % 16 == 0
        in_specs.append(pl.BlockSpec((None, block_rows, cols), lambda i: (0, i, 0)))
        out_specs.append(pl.BlockSpec((block_rows, cols), lambda i: (i, 0)))
        out_shape.append(jax.ShapeDtypeStruct((rows, cols), BF16))
    res = pl.pallas_call(
        _ssm_prep_kernel,
        grid=(n_half,),
        in_specs=in_specs,
        out_specs=out_specs,
        out_shape=out_shape,
        compiler_params=pltpu.CompilerParams(
            dimension_semantics=("parallel",), vmem_limit_bytes=V7X_VMEM_LIMIT_BYTES),
        name="ssm_prep",
    )(*ins, *weights_f32)
    return res[:4], res[4:]


def _pool_mixer(pbuf_ref, sb, time0, wpool_ref, pscale_ref):
    row = lax.broadcasted_iota(jnp.int32, (BLOCK_ROWS, POOL_GROUP), 0)
    n_pos = time0 + row // BATCH + 1
    y_pool = []
    for gi, w in enumerate(POOL_WINDOWS):
        x = pbuf_ref[sb, :, gi * POOL_GROUP:(gi + 1) * POOL_GROUP]
        s = x
        span = 1
        while span < w:
            s = s + pltpu.roll(s, span * BATCH, 0)
            span *= 2
        mean = s[POOL_HIST_ROWS:, :] / jnp.minimum(n_pos, w).astype(F32)
        d = mean - x[POOL_HIST_ROWS:, :]
        y_pool.append(jnp.dot(d.astype(BF16), wpool_ref[gi * POOL_GROUP:(gi + 1) * POOL_GROUP, :],
                              preferred_element_type=F32))
    return jnp.concatenate(y_pool, axis=1) * pscale_ref[...]


def _swap_time_pairs(x):
    n = x.shape[0]
    first_half = lax.broadcasted_iota(jnp.int32, x.shape, 0) % V7X_SUBLANES < BATCH
    return jnp.where(first_half, pltpu.roll(x, n - BATCH, 0), pltpu.roll(x, BATCH, 0))


def _scan_block(bu_ref, sb, j, w, a_r, a_i, lo):
    for m in range(BLOCK_ROWS // V7X_SUBLANES):
        rows = slice(m * V7X_SUBLANES, (m + 1) * V7X_SUBLANES)
        re = bu_ref[sb, j, rows, 0:STATE_PER_HALF]
        im = bu_ref[sb, j, rows, STATE_PER_HALF:]
        x0 = jnp.where(lo, re, im)
        x1 = jnp.where(lo, im, re)
        v0 = a_r * pltpu.roll(w, BATCH, 0) + a_i * w + x0
        w = a_r * pltpu.roll(v0, BATCH, 0) - a_i * v0 + x1
        bu_ref[sb, j, rows, 0:STATE_PER_HALF] = jnp.where(lo, v0, w)
        bu_ref[sb, j, rows, STATE_PER_HALF:] = jnp.where(lo, w, v0)
    return w


def _ffn_block(h, gffn_ref, wg_ref, wu_ref, wd_ref):
    rstd = lax.rsqrt(jnp.mean(h * h, axis=-1, keepdims=True) + RMS_EPS)
    hg = (h * gffn_ref[...]).astype(BF16)
    out = h
    for lo_col, hi_col in zip(FF_CHUNK_EDGES[:-1], FF_CHUNK_EDGES[1:]):
        g = jnp.dot(hg, wg_ref[:, lo_col:hi_col], preferred_element_type=F32) * rstd
        u = jnp.dot(hg, wu_ref[:, lo_col:hi_col], preferred_element_type=F32)
        a = (g * _sigmoid(g) * rstd * u).astype(BF16)
        out = out + jnp.dot(a, wd_ref[lo_col:hi_col, :], preferred_element_type=F32)
    return out


def _layer_kernel(*refs, first, final):
    refs = list(refs)
    h_ref = refs.pop(0)
    (gmix_ref, pscale_ref, wb_ref, ar_ref, ai_ref, wc_ref, dskip_ref, bglu_ref,
     gffn_ref) = [refs.pop(0) for _ in range(N_SMALL_PARAMS)]
    win_ref, wpool_ref, wglu_ref, wout_ref, wg_ref, wu_ref, wd_ref = [
        refs.pop(0) for _ in BIG_WEIGHTS]
    gfin_ref = refs.pop(0)
    next_f32 = [] if final else [refs.pop(0) for _ in BIG_WEIGHTS]
    o_ref = refs.pop(0)
    next_bf16 = [] if final else [refs.pop(0) for _ in BIG_WEIGHTS]
    hist_ref, state_ref, pbuf_ref, bu_ref = [refs.pop(0) for _ in range(4)]
    reorder_ref = refs
    step = pl.program_id(0)

    for src, dst in zip(next_f32, next_bf16):
        dst[...] = src[...].astype(BF16)

    @pl.when(step == 0)
    def _():
        hist_ref[...] = jnp.zeros_like(hist_ref)
        state_ref[...] = jnp.zeros_like(state_ref)

    lo = lax.broadcasted_iota(jnp.int32, (V7X_SUBLANES, STATE_PER_HALF), 0) < BATCH
    v = [state_ref[j] for j in range(SSM_HALVES)]
    hist = hist_ref[...]
    heads = []
    for sb in range(SUB_BLOCKS):
        rows = slice(sb * BLOCK_ROWS, (sb + 1) * BLOCK_ROWS)
        if first:
            hbuf_ref = reorder_ref[0]
            t0 = sb * TIME_BLOCK
            for k in range(LANE_SLABS):
                for b in range(BATCH):
                    hbuf_ref[k, pl.ds(b, TIME_BLOCK, stride=BATCH), :] = (
                        h_ref[b, t0:t0 + TIME_BLOCK, k * V7X_LANES:(k + 1) * V7X_LANES])
            h = jnp.concatenate([hbuf_ref[k] for k in range(LANE_SLABS)], axis=1)
        else:
            h = h_ref[rows, :]
        rstd = lax.rsqrt(jnp.mean(h * h, axis=-1, keepdims=True) + RMS_EPS)
        u = jnp.dot((h * gmix_ref[...]).astype(BF16), win_ref[...],
                    preferred_element_type=F32) * rstd

        us = u[:, D_POOL:]
        usb = us.astype(BF16)
        usb_swapped = _swap_time_pairs(us).astype(BF16)
        for j in range(SSM_HALVES):
            cols = slice(j * CH_PER_HALF, (j + 1) * CH_PER_HALF)
            bu_ref[sb, j, :, 0:STATE_PER_HALF] = jnp.dot(
                usb[:, cols], wb_ref[j, :, 0:STATE_PER_HALF], preferred_element_type=F32)
            bu_ref[sb, j, :, STATE_PER_HALF:] = jnp.dot(
                usb_swapped[:, cols], wb_ref[j, :, STATE_PER_HALF:], preferred_element_type=F32)

        up = u[:, :D_POOL]
        pbuf_ref[sb, 0:POOL_HIST_ROWS, :] = hist
        pbuf_ref[sb, POOL_HIST_ROWS:, :] = up
        hist = up[BLOCK_ROWS - POOL_HIST_ROWS:, :]
        y_pool = _pool_mixer(pbuf_ref, sb, step * STEP_TIME + sb * TIME_BLOCK, wpool_ref, pscale_ref)
        heads.append((h, us, y_pool))

    for sb in range(SUB_BLOCKS):
        for j in range(SSM_HALVES):
            v[j] = _scan_block(bu_ref, sb, j, v[j], ar_ref[j], ai_ref[j], lo)

    for sb in range(SUB_BLOCKS):
        rows = slice(sb * BLOCK_ROWS, (sb + 1) * BLOCK_ROWS)
        h, us, y_pool = heads[sb]
        y_halves = []
        for j in range(SSM_HALVES):
            y_re = jnp.dot(bu_ref[sb, j, :, 0:STATE_PER_HALF].astype(BF16),
                           wc_ref[j, 0:STATE_PER_HALF, :], preferred_element_type=F32)
            y_im = jnp.dot(bu_ref[sb, j, :, STATE_PER_HALF:].astype(BF16),
                           wc_ref[j, STATE_PER_HALF:, :], preferred_element_type=F32)
            y_halves.append(y_re + _swap_time_pairs(y_im))
        y = jnp.concatenate(y_halves, axis=1) + dskip_ref[...] * us
        y = 0.5 * y * (1.0 + jnp.tanh(math.sqrt(2.0 / math.pi) * (y + 0.044715 * (y * y * y))))
        z = jnp.dot(y.astype(BF16), wglu_ref[...], preferred_element_type=F32) + bglu_ref[...]
        y_ssm = y * _sigmoid(z)

        cat = jnp.concatenate([y_pool, y_ssm], axis=1).astype(BF16)
        h = h + jnp.dot(cat, wout_ref[...], preferred_element_type=F32)

        out = _ffn_block(h, gffn_ref, wg_ref, wu_ref, wd_ref)
        if final:
            obuf_ref = reorder_ref[-1]
            out = _rmsnorm(out, gfin_ref[...])
            t0 = sb * TIME_BLOCK
            for k in range(LANE_SLABS):
                obuf_ref[k] = out[:, k * V7X_LANES:(k + 1) * V7X_LANES]
                for b in range(BATCH):
                    o_ref[b, t0:t0 + TIME_BLOCK, k * V7X_LANES:(k + 1) * V7X_LANES] = (
                        obuf_ref[k, pl.ds(b, TIME_BLOCK, stride=BATCH), :])
        else:
            o_ref[rows, :] = out
    hist_ref[...] = hist
    for j in range(SSM_HALVES):
        state_ref[j] = v[j]


def _layer_spec(a, layer):
    tail = (0,) * (a.ndim - 1)
    return pl.BlockSpec((None,) + a.shape[1:], lambda i: (layer,) + tail,
                        pipeline_mode=pl.Buffered(1))


def _resident_spec(a):
    zeros = (0,) * a.ndim
    return pl.BlockSpec(a.shape, lambda i: zeros, pipeline_mode=pl.Buffered(1))


def _cast_specs(w, layer):
    rows, cols = w.shape[1:]
    block_rows = CAST_BLOCK_ROWS if rows % N_STEPS or rows // N_STEPS % 16 else rows // N_STEPS
    last = rows // block_rows - 1
    assert rows % block_rows == 0 and last < N_STEPS
    src = pl.BlockSpec((None, block_rows, cols), lambda i: (layer, jnp.minimum(i, last), 0))
    dst = pl.BlockSpec((block_rows, cols), lambda i: (jnp.minimum(i, last), 0))
    return src, dst, jax.ShapeDtypeStruct((rows, cols), BF16)


def _layer(h, layer, small, weights, gfin, weights_f32):
    first = layer == 0
    final = layer == DEPTH - 1
    row_spec = pl.BlockSpec((STEP_ROWS, D_MODEL), lambda i: (i, 0))
    btd_spec = pl.BlockSpec((BATCH, STEP_TIME, D_MODEL), lambda i: (0, i, 0))
    scratch = [
        pltpu.VMEM((POOL_HIST_ROWS, D_POOL), F32),
        pltpu.VMEM((SSM_HALVES, V7X_SUBLANES, STATE_PER_HALF), F32),
        pltpu.VMEM((SUB_BLOCKS, BLOCK_ROWS + POOL_HIST_ROWS, D_POOL), F32),
        pltpu.VMEM((SUB_BLOCKS, SSM_HALVES, BLOCK_ROWS, RE_IM * STATE_PER_HALF), F32),
    ]
    if first:
        scratch.append(pltpu.VMEM((LANE_SLABS, BLOCK_ROWS, V7X_LANES), F32))
    if final:
        scratch.append(pltpu.VMEM((LANE_SLABS, BLOCK_ROWS, V7X_LANES), F32))
    in_specs = ([btd_spec if first else row_spec] + [_layer_spec(c, layer) for c in small]
                + [_resident_spec(w) for w in weights] + [_resident_spec(gfin)])
    operands = [h, *small, *weights, gfin]
    if final:
        out_specs = [btd_spec]
        out_shape = [jax.ShapeDtypeStruct((BATCH, SEQ, D_MODEL), F32)]
    else:
        out_specs = [row_spec]
        out_shape = [jax.ShapeDtypeStruct((SEQ * BATCH, D_MODEL), F32)]
        for w in weights_f32:
            src, dst, shape = _cast_specs(w, layer + 1)
            in_specs.append(src)
            operands.append(w)
            out_specs.append(dst)
            out_shape.append(shape)
    outs = pl.pallas_call(
        functools.partial(_layer_kernel, first=first, final=final),
        grid=(N_STEPS,),
        in_specs=in_specs,
        out_specs=out_specs,
        out_shape=out_shape,
        scratch_shapes=scratch,
        compiler_params=pltpu.CompilerParams(
            dimension_semantics=("arbitrary",), vmem_limit_bytes=V7X_VMEM_LIMIT_BYTES),
        name="layer_%d" % layer,
    )(*operands)
    return outs[0], outs[1:]


def kernel(x, norm_mix, w_in, w_pool, pool_scale, lam_re, lam_im, log_dt, b_re, b_im, c_re, c_im,
           d_skip, w_glu, b_glu, w_out, norm_ffn, w_gate, w_up, w_down, norm_final):
    assert x.shape == (BATCH, SEQ, D_MODEL) and x.dtype == F32

    weights_f32 = (w_in, w_pool.reshape(DEPTH, D_POOL, POOL_GROUP), w_glu, w_out, w_gate, w_up,
                   w_down)
    (abar_r, abar_i, wb, wc), weights = _ssm_prep(lam_re, lam_im, log_dt, b_re, b_im, c_re, c_im,
                                                  weights_f32)
    per_row = (DEPTH, SSM_HALVES, GROUPS_PER_HALF, SSM_GROUP, SSM_STATE)
    abar_r = abar_r.reshape(per_row)[:, :, :, 0, :].reshape(DEPTH, SSM_HALVES, 1, STATE_PER_HALF)
    abar_i = abar_i.reshape(per_row)[:, :, :, 0, :].reshape(DEPTH, SSM_HALVES, 1, STATE_PER_HALF)
    a_r = jnp.broadcast_to(abar_r, (DEPTH, SSM_HALVES, V7X_SUBLANES, STATE_PER_HALF))
    sign = jnp.where(jnp.arange(V7X_SUBLANES) < BATCH, -1.0, 1.0).astype(F32)[None, None, :, None]
    a_i = sign * abar_i
    wb = wb.reshape(DEPTH, SSM_HALVES, CH_PER_HALF, RE_IM * STATE_PER_HALF)
    wc = wc.reshape(DEPTH, SSM_HALVES, RE_IM * STATE_PER_HALF, CH_PER_HALF)

    small = (
        norm_mix.reshape(DEPTH, 1, D_MODEL), pool_scale.reshape(DEPTH, 1, D_POOL), wb, a_r, a_i, wc,
        d_skip.reshape(DEPTH, 1, D_SSM), b_glu.reshape(DEPTH, 1, D_SSM),
        norm_ffn.reshape(DEPTH, 1, D_MODEL))
    assert len(small) == N_SMALL_PARAMS
    gfin = norm_final.reshape(1, D_MODEL)

    h = x
    for layer in range(DEPTH):
        h, weights = _layer(h, layer, small, weights, gfin, weights_f32)
    return h
```

```python
import functools
import math

import jax
import jax.numpy as jnp
from jax import lax
from jax.experimental import pallas as pl
from jax.experimental.pallas import tpu as pltpu

F32 = jnp.float32
BF16 = jnp.bfloat16

D_MODEL = 1024
BATCH = 4
SEQ = 4096
DEPTH = 4
D_POOL = 512
D_SSM = 512
POOL_WINDOWS = (2, 4, 8, 16)
POOL_GROUP = 128
SSM_GROUP = 16
N_SSM_GROUPS = 32
SSM_STATE = 64
D_FF = 2816
RMS_EPS = 1e-6

V7X_SUBLANES = 8
V7X_LANES = 128
V7X_MXU_DIM = 256
V7X_VMEM_LIMIT_BYTES = 60 * 1024 * 1024

SSM_HALVES = D_SSM // V7X_MXU_DIM
GROUPS_PER_HALF = N_SSM_GROUPS // SSM_HALVES
CH_PER_HALF = GROUPS_PER_HALF * SSM_GROUP
STATE_PER_HALF = GROUPS_PER_HALF * SSM_STATE
RE_IM = 2

TIME_BLOCK = 64
BLOCK_ROWS = TIME_BLOCK * BATCH
SUB_BLOCKS = 2
STEP_TIME = SUB_BLOCKS * TIME_BLOCK
STEP_ROWS = SUB_BLOCKS * BLOCK_ROWS
N_STEPS = SEQ // STEP_TIME
POOL_HIST_STEPS = max(POOL_WINDOWS)
POOL_HIST_ROWS = POOL_HIST_STEPS * BATCH
LANE_SLABS = D_MODEL // V7X_LANES
FF_CHUNK_EDGES = (0, 6 * V7X_MXU_DIM, D_FF)
BIG_WEIGHTS = ("w_in", "w_pool", "w_glu", "w_out", "w_gate", "w_up", "w_down")
N_SMALL_PARAMS = 9
CAST_BLOCK_ROWS = 128

assert BATCH * RE_IM == V7X_SUBLANES, "the scan tile layout needs batch*2 == 8 sublanes"
assert POOL_HIST_ROWS <= BLOCK_ROWS and D_FF % V7X_MXU_DIM == 0


def _rmsnorm(x, g):
    return x * lax.rsqrt(jnp.mean(x * x, axis=-1, keepdims=True) + RMS_EPS) * g


def _sigmoid(x):
    return 0.5 + 0.5 * jnp.tanh(0.5 * x)


def _ssm_prep_kernel(lr_ref, li_ref, ldt_ref, br_ref, bi_ref, ctr_ref, cti_ref, *refs):
    n_w = len(BIG_WEIGHTS)
    first_f32 = refs[:n_w]
    ar_ref, ai_ref, wb_ref, wc_ref = refs[n_w:n_w + 4]
    first_bf16 = refs[n_w + 4:]
    for src, dst in zip(first_f32, first_bf16):
        dst[...] = src[...].astype(BF16)

    lr = lr_ref[...]
    li = li_ref[...]
    dt = jnp.exp(ldt_ref[...])
    mag = jnp.exp(lr * dt)
    abar_r = mag * jnp.cos(li * dt)
    abar_i = mag * jnp.sin(li * dt)
    den = lr * lr + li * li
    nr = abar_r - 1.0
    ni = abar_i
    coef_r = (nr * lr + ni * li) / den
    coef_i = (ni * lr - nr * li) / den
    br = br_ref[...]
    bi = bi_ref[...]
    ar_ref[...] = abar_r
    ai_ref[...] = abar_i
    bbar_r = coef_r * br - coef_i * bi
    bbar_i = coef_r * bi + coef_i * br

    def iota(shape, axis):
        return lax.broadcasted_iota(jnp.int32, shape, axis)

    tile_in = (iota((SSM_STATE, STATE_PER_HALF), 1) % SSM_STATE
               == iota((SSM_STATE, STATE_PER_HALF), 0)).astype(BF16)
    same_group_in = (iota((CH_PER_HALF, STATE_PER_HALF), 0) // SSM_GROUP
                     == iota((CH_PER_HALF, STATE_PER_HALF), 1) // SSM_STATE)
    for part, bbar in enumerate((bbar_r, bbar_i)):
        tiled = jnp.dot(bbar.astype(BF16), tile_in, preferred_element_type=F32)
        wb_ref[:, part * STATE_PER_HALF:(part + 1) * STATE_PER_HALF] = (
            jnp.where(same_group_in, tiled, 0.0).astype(BF16))

    tile_out = (iota((STATE_PER_HALF, SSM_STATE), 0) % SSM_STATE
                == iota((STATE_PER_HALF, SSM_STATE), 1)).astype(BF16)
    same_group_out = (iota((STATE_PER_HALF, CH_PER_HALF), 0) // SSM_STATE
                      == iota((STATE_PER_HALF, CH_PER_HALF), 1) // SSM_GROUP)
    for part, ct in enumerate((ctr_ref[...], -cti_ref[...])):
        tiled = jnp.dot(tile_out, ct.astype(BF16), preferred_element_type=F32)
        wc_ref[part * STATE_PER_HALF:(part + 1) * STATE_PER_HALF, :] = (
            jnp.where(same_group_out, tiled, 0.0).astype(BF16))


def _ssm_prep(lam_re, lam_im, log_dt, b_re, b_im, c_re, c_im, weights_f32):
    n_half = DEPTH * SSM_HALVES

    def per_channel(a):
        a = jnp.broadcast_to(a[:, :, None, :], (DEPTH, N_SSM_GROUPS, SSM_GROUP, SSM_STATE))
        return a.reshape(n_half, CH_PER_HALF, SSM_STATE)

    def channel_rows(a):
        return jnp.transpose(a, (0, 1, 3, 2)).reshape(n_half, CH_PER_HALF, SSM_STATE)

    def state_rows(a):
        return jnp.transpose(a.reshape(n_half, CH_PER_HALF, SSM_STATE), (0, 2, 1))

    ins = (per_channel(lam_re), per_channel(lam_im),
           per_channel(jnp.broadcast_to(log_dt[..., None], lam_re.shape)),
           channel_rows(b_re), channel_rows(b_im), state_rows(c_re), state_rows(c_im))

    def spec(a):
        return pl.BlockSpec((None,) + a.shape[1:], lambda i: (i, 0, 0))

    outs = (jax.ShapeDtypeStruct((n_half, CH_PER_HALF, SSM_STATE), F32),
            jax.ShapeDtypeStruct((n_half, CH_PER_HALF, SSM_STATE), F32),
            jax.ShapeDtypeStruct((n_half, CH_PER_HALF, RE_IM * STATE_PER_HALF), BF16),
            jax.ShapeDtypeStruct((n_half, RE_IM * STATE_PER_HALF, CH_PER_HALF), BF16))
    in_specs = [spec(a) for a in ins]
    out_specs = [spec(o) for o in outs]
    out_shape = list(outs)
    for w in weights_f32:
        rows, cols = w.shape[1:]
        block_rows = rows // n_half
        assert rows % n_half == 0 and block_rows % 16 == 0
        in_specs.append(pl.BlockSpec((None, block_rows, cols), lambda i: (0, i, 0)))
        out_specs.append(pl.BlockSpec((block_rows, cols), lambda i: (i, 0)))
        out_shape.append(jax.ShapeDtypeStruct((rows, cols), BF16))
    res = pl.pallas_call(
        _ssm_prep_kernel,
        grid=(n_half,),
        in_specs=in_specs,
        out_specs=out_specs,
        out_shape=out_shape,
        compiler_params=pltpu.CompilerParams(
            dimension_semantics=("parallel",), vmem_limit_bytes=V7X_VMEM_LIMIT_BYTES),
        name="ssm_prep",
    )(*ins, *weights_f32)
    return res[:4], res[4:]


def _pool_mixer(pbuf_ref, sb, time0, wpool_ref, pscale_ref):
    row = lax.broadcasted_iota(jnp.int32, (BLOCK_ROWS, 1), 0)
    n_pos = time0 + row // BATCH + 1
    y_pool = []
    for gi, w in enumerate(POOL_WINDOWS):
        x = pbuf_ref[sb, :, gi * POOL_GROUP:(gi + 1) * POOL_GROUP]
        s = x
        span = 1
        while span < w:
            s = s + pltpu.roll(s, span * BATCH, 0)
            span *= 2
        mean = s[POOL_HIST_ROWS:, :] * (1.0 / jnp.minimum(n_pos, w).astype(F32))
        d = mean - x[POOL_HIST_ROWS:, :]
        y_pool.append(jnp.dot(d.astype(BF16), wpool_ref[gi * POOL_GROUP:(gi + 1) * POOL_GROUP, :],
                              preferred_element_type=F32))
    return jnp.concatenate(y_pool, axis=1) * pscale_ref[...]


def _swap_time_pairs(x):
    n, c = x.shape
    tiles = x.reshape(n // V7X_SUBLANES, V7X_SUBLANES, c)
    return pltpu.roll(tiles, BATCH, 1).reshape(n, c)


def _scan_block(bu_ref, sb, j, w, a_r, a_i, lo):
    for m in range(BLOCK_ROWS // V7X_SUBLANES):
        rows = slice(m * V7X_SUBLANES, (m + 1) * V7X_SUBLANES)
        re = bu_ref[sb, j, rows, 0:STATE_PER_HALF]
        im = bu_ref[sb, j, rows, STATE_PER_HALF:]
        x0 = jnp.where(lo, re, im)
        x1 = jnp.where(lo, im, re)
        v0 = a_r * pltpu.roll(w, BATCH, 0) + a_i * w + x0
        w = a_r * pltpu.roll(v0, BATCH, 0) - a_i * v0 + x1
        bu_ref[sb, j, rows, 0:STATE_PER_HALF] = jnp.where(lo, v0, w)
        bu_ref[sb, j, rows, STATE_PER_HALF:] = jnp.where(lo, w, v0)
    return w


def _ffn_block(h, gffn_ref, wg_ref, wu_ref, wd_ref):
    rstd = lax.rsqrt(jnp.mean(h * h, axis=-1, keepdims=True) + RMS_EPS)
    hg = (h * gffn_ref[...]).astype(BF16)
    out = h
    for lo_col, hi_col in zip(FF_CHUNK_EDGES[:-1], FF_CHUNK_EDGES[1:]):
        g = jnp.dot(hg, wg_ref[:, lo_col:hi_col], preferred_element_type=F32) * rstd
        u = jnp.dot(hg, wu_ref[:, lo_col:hi_col], preferred_element_type=F32)
        a = (g * _sigmoid(g) * rstd * u).astype(BF16)
        out = out + jnp.dot(a, wd_ref[lo_col:hi_col, :], preferred_element_type=F32)
    return out


def _layer_kernel(*refs, first, final):
    refs = list(refs)
    h_ref = refs.pop(0)
    (gmix_ref, pscale_ref, wb_ref, ar_ref, ai_ref, wc_ref, dskip_ref, bglu_ref,
     gffn_ref) = [refs.pop(0) for _ in range(N_SMALL_PARAMS)]
    win_ref, wpool_ref, wglu_ref, wout_ref, wg_ref, wu_ref, wd_ref = [
        refs.pop(0) for _ in BIG_WEIGHTS]
    gfin_ref = refs.pop(0)
    next_f32 = [] if final else [refs.pop(0) for _ in BIG_WEIGHTS]
    o_ref = refs.pop(0)
    next_bf16 = [] if final else [refs.pop(0) for _ in BIG_WEIGHTS]
    hist_ref, state_ref, pbuf_ref, bu_ref = [refs.pop(0) for _ in range(4)]
    reorder_ref = refs
    step = pl.program_id(0)

    @pl.when(step == 0)
    def _():
        hist_ref[...] = jnp.zeros_like(hist_ref)
        state_ref[...] = jnp.zeros_like(state_ref)

    lo = lax.broadcasted_iota(jnp.int32, (V7X_SUBLANES, STATE_PER_HALF), 0) < BATCH
    v = [state_ref[j] for j in range(SSM_HALVES)]
    hist = hist_ref[...]
    inproj = []
    for sb in range(SUB_BLOCKS):
        rows = slice(sb * BLOCK_ROWS, (sb + 1) * BLOCK_ROWS)
        if first:
            hbuf_ref = reorder_ref[0]
            t0 = sb * TIME_BLOCK
            for k in range(LANE_SLABS):
                for b in range(BATCH):
                    hbuf_ref[k, pl.ds(b, TIME_BLOCK, stride=BATCH), :] = (
                        h_ref[b, t0:t0 + TIME_BLOCK, k * V7X_LANES:(k + 1) * V7X_LANES])
            h = jnp.concatenate([hbuf_ref[k] for k in range(LANE_SLABS)], axis=1)
        else:
            h = h_ref[rows, :]
        rstd = lax.rsqrt(jnp.mean(h * h, axis=-1, keepdims=True) + RMS_EPS)
        u = jnp.dot((h * gmix_ref[...]).astype(BF16), win_ref[...],
                    preferred_element_type=F32) * rstd
        inproj.append((h, u))

    heads = []
    for sb in range(SUB_BLOCKS):
        h, u = inproj[sb]
        us = u[:, D_POOL:]
        usb = us.astype(BF16)
        usb_swapped = _swap_time_pairs(us).astype(BF16)
        for j in range(SSM_HALVES):
            cols = slice(j * CH_PER_HALF, (j + 1) * CH_PER_HALF)
            bu_ref[sb, j, :, 0:STATE_PER_HALF] = jnp.dot(
                usb[:, cols], wb_ref[j, :, 0:STATE_PER_HALF], preferred_element_type=F32)
            bu_ref[sb, j, :, STATE_PER_HALF:] = jnp.dot(
                usb_swapped[:, cols], wb_ref[j, :, STATE_PER_HALF:], preferred_element_type=F32)

        up = u[:, :D_POOL]
        pbuf_ref[sb, 0:POOL_HIST_ROWS, :] = hist
        pbuf_ref[sb, POOL_HIST_ROWS:, :] = up
        hist = up[BLOCK_ROWS - POOL_HIST_ROWS:, :]
        y_pool = _pool_mixer(pbuf_ref, sb, step * STEP_TIME + sb * TIME_BLOCK, wpool_ref, pscale_ref)
        heads.append((h, us, y_pool))

    for sb in range(SUB_BLOCKS):
        for j in range(SSM_HALVES):
            v[j] = _scan_block(bu_ref, sb, j, v[j], ar_ref[j], ai_ref[j], lo)

    for sb in range(SUB_BLOCKS):
        rows = slice(sb * BLOCK_ROWS, (sb + 1) * BLOCK_ROWS)
        h, us, y_pool = heads[sb]
        y_halves = []
        for j in range(SSM_HALVES):
            y_re = jnp.dot(bu_ref[sb, j, :, 0:STATE_PER_HALF].astype(BF16),
                           wc_ref[j, 0:STATE_PER_HALF, :], preferred_element_type=F32)
            y_im = jnp.dot(bu_ref[sb, j, :, STATE_PER_HALF:].astype(BF16),
                           wc_ref[j, STATE_PER_HALF:, :], preferred_element_type=F32)
            y_halves.append(y_re + _swap_time_pairs(y_im))
        y = jnp.concatenate(y_halves, axis=1) + dskip_ref[...] * us
        y = 0.5 * y * (1.0 + jnp.tanh(math.sqrt(2.0 / math.pi) * (y + 0.044715 * (y * y * y))))
        z = jnp.dot(y.astype(BF16), wglu_ref[...], preferred_element_type=F32) + bglu_ref[...]
        y_ssm = y * _sigmoid(z)

        cat = jnp.concatenate([y_pool, y_ssm], axis=1).astype(BF16)
        h = h + jnp.dot(cat, wout_ref[...], preferred_element_type=F32)

        out = _ffn_block(h, gffn_ref, wg_ref, wu_ref, wd_ref)
        if final:
            obuf_ref = reorder_ref[-1]
            out = _rmsnorm(out, gfin_ref[...])
            t0 = sb * TIME_BLOCK
            for k in range(LANE_SLABS):
                obuf_ref[k] = out[:, k * V7X_LANES:(k + 1) * V7X_LANES]
                for b in range(BATCH):
                    o_ref[b, t0:t0 + TIME_BLOCK, k * V7X_LANES:(k + 1) * V7X_LANES] = (
                        obuf_ref[k, pl.ds(b, TIME_BLOCK, stride=BATCH), :])
        else:
            o_ref[rows, :] = out
    hist_ref[...] = hist
    for j in range(SSM_HALVES):
        state_ref[j] = v[j]

    for src, dst in zip(next_f32, next_bf16):
        dst[...] = src[...].astype(BF16)


def _layer_spec(a, layer):
    tail = (0,) * (a.ndim - 1)
    return pl.BlockSpec((None,) + a.shape[1:], lambda i: (layer,) + tail,
                        pipeline_mode=pl.Buffered(1))


def _resident_spec(a):
    zeros = (0,) * a.ndim
    return pl.BlockSpec(a.shape, lambda i: zeros, pipeline_mode=pl.Buffered(1))


def _cast_specs(w, layer):
    rows, cols = w.shape[1:]
    block_rows = CAST_BLOCK_ROWS if rows % N_STEPS or rows // N_STEPS % 16 else rows // N_STEPS
    last = rows // block_rows - 1
    assert rows % block_rows == 0 and last < N_STEPS
    src = pl.BlockSpec((None, block_rows, cols), lambda i: (layer, jnp.minimum(i, last), 0))
    dst = pl.BlockSpec((block_rows, cols), lambda i: (jnp.minimum(i, last), 0))
    return src, dst, jax.ShapeDtypeStruct((rows, cols), BF16)


def _layer(h, layer, small, weights, gfin, weights_f32):
    first = layer == 0
    final = layer == DEPTH - 1
    row_spec = pl.BlockSpec((STEP_ROWS, D_MODEL), lambda i: (i, 0))
    btd_spec = pl.BlockSpec((BATCH, STEP_TIME, D_MODEL), lambda i: (0, i, 0))
    scratch = [
        pltpu.VMEM((POOL_HIST_ROWS, D_POOL), F32),
        pltpu.VMEM((SSM_HALVES, V7X_SUBLANES, STATE_PER_HALF), F32),
        pltpu.VMEM((SUB_BLOCKS, BLOCK_ROWS + POOL_HIST_ROWS, D_POOL), F32),
        pltpu.VMEM((SUB_BLOCKS, SSM_HALVES, BLOCK_ROWS, RE_IM * STATE_PER_HALF), F32),
    ]
    if first:
        scratch.append(pltpu.VMEM((LANE_SLABS, BLOCK_ROWS, V7X_LANES), F32))
    if final:
        scratch.append(pltpu.VMEM((LANE_SLABS, BLOCK_ROWS, V7X_LANES), F32))
    in_specs = ([btd_spec if first else row_spec] + [_layer_spec(c, layer) for c in small]
                + [_resident_spec(w) for w in weights] + [_resident_spec(gfin)])
    operands = [h, *small, *weights, gfin]
    if final:
        out_specs = [btd_spec]
        out_shape = [jax.ShapeDtypeStruct((BATCH, SEQ, D_MODEL), F32)]
    else:
        out_specs = [row_spec]
        out_shape = [jax.ShapeDtypeStruct((SEQ * BATCH, D_MODEL), F32)]
        for w in weights_f32:
            src, dst, shape = _cast_specs(w, layer + 1)
            in_specs.append(src)
            operands.append(w)
            out_specs.append(dst)
            out_shape.append(shape)
    outs = pl.pallas_call(
        functools.partial(_layer_kernel, first=first, final=final),
        grid=(N_STEPS,),
        in_specs=in_specs,
        out_specs=out_specs,
        out_shape=out_shape,
        scratch_shapes=scratch,
        compiler_params=pltpu.CompilerParams(
            dimension_semantics=("arbitrary",), vmem_limit_bytes=V7X_VMEM_LIMIT_BYTES),
        name="layer_%d" % layer,
    )(*operands)
    return outs[0], outs[1:]


def kernel(x, norm_mix, w_in, w_pool, pool_scale, lam_re, lam_im, log_dt, b_re, b_im, c_re, c_im,
           d_skip, w_glu, b_glu, w_out, norm_ffn, w_gate, w_up, w_down, norm_final):
    assert x.shape == (BATCH, SEQ, D_MODEL) and x.dtype == F32

    weights_f32 = (w_in, w_pool.reshape(DEPTH, D_POOL, POOL_GROUP), w_glu, w_out, w_gate, w_up,
                   w_down)
    (abar_r, abar_i, wb, wc), weights = _ssm_prep(lam_re, lam_im, log_dt, b_re, b_im, c_re, c_im,
                                                  weights_f32)
    per_row = (DEPTH, SSM_HALVES, GROUPS_PER_HALF, SSM_GROUP, SSM_STATE)
    abar_r = abar_r.reshape(per_row)[:, :, :, 0, :].reshape(DEPTH, SSM_HALVES, 1, STATE_PER_HALF)
    abar_i = abar_i.reshape(per_row)[:, :, :, 0, :].reshape(DEPTH, SSM_HALVES, 1, STATE_PER_HALF)
    a_r = jnp.broadcast_to(abar_r, (DEPTH, SSM_HALVES, V7X_SUBLANES, STATE_PER_HALF))
    sign = jnp.where(jnp.arange(V7X_SUBLANES) < BATCH, -1.0, 1.0).astype(F32)[None, None, :, None]
    a_i = sign * abar_i
    wb = wb.reshape(DEPTH, SSM_HALVES, CH_PER_HALF, RE_IM * STATE_PER_HALF)
    wc = wc.reshape(DEPTH, SSM_HALVES, RE_IM * STATE_PER_HALF, CH_PER_HALF)

    small = (
        norm_mix.reshape(DEPTH, 1, D_MODEL), pool_scale.reshape(DEPTH, 1, D_POOL), wb, a_r, a_i, wc,
        d_skip.reshape(DEPTH, 1, D_SSM), b_glu.reshape(DEPTH, 1, D_SSM),
        norm_ffn.reshape(DEPTH, 1, D_MODEL))
    assert len(small) == N_SMALL_PARAMS
    gfin = norm_final.reshape(1, D_MODEL)

    h = x
    for layer in range(DEPTH):
        h, weights = _layer(h, layer, small, weights, gfin, weights_f32)
    return h
```

```python
import functools
import math

import jax
import jax.numpy as jnp
from jax import lax
from jax.experimental import pallas as pl
from jax.experimental.pallas import tpu as pltpu

F32 = jnp.float32
BF16 = jnp.bfloat16

D_MODEL = 1024
BATCH = 4
SEQ = 4096
DEPTH = 4
D_POOL = 512
D_SSM = 512
POOL_WINDOWS = (2, 4, 8, 16)
POOL_GROUP = 128
SSM_GROUP = 16
N_SSM_GROUPS = 32
SSM_STATE = 64
D_FF = 2816
RMS_EPS = 1e-6
GELU_C = math.sqrt(2.0 / math.pi)

V7X_SUBLANES = 8
V7X_LANES = 128
V7X_MXU_DIM = 256
V7X_VMEM_LIMIT_BYTES = 60 * 1024 * 1024

SSM_HALVES = D_SSM // V7X_MXU_DIM
GROUPS_PER_HALF = N_SSM_GROUPS // SSM_HALVES
CH_PER_HALF = GROUPS_PER_HALF * SSM_GROUP
STATE_PER_HALF = GROUPS_PER_HALF * SSM_STATE
RE_IM = 2

TIME_BLOCK = 64
BLOCK_ROWS = TIME_BLOCK * BATCH
SUB_BLOCKS = 2
STEP_TIME = SUB_BLOCKS * TIME_BLOCK
STEP_ROWS = SUB_BLOCKS * BLOCK_ROWS
N_STEPS = SEQ // STEP_TIME
POOL_HIST_STEPS = max(POOL_WINDOWS)
POOL_HIST_ROWS = POOL_HIST_STEPS * BATCH
LANE_SLABS = D_MODEL // V7X_LANES
FF_CHUNK_EDGES = (0, 6 * V7X_MXU_DIM, D_FF)
BIG_WEIGHTS = ("w_in", "w_pool", "w_glu", "w_out", "w_gate", "w_up", "w_down")
N_SMALL_PARAMS = 9
CAST_BLOCK_ROWS = 128

assert BATCH * RE_IM == V7X_SUBLANES, "the scan tile layout needs batch*2 == 8 sublanes"
assert POOL_HIST_ROWS <= BLOCK_ROWS and D_FF % V7X_MXU_DIM == 0


def _rmsnorm(x, g):
    return x * lax.rsqrt(jnp.mean(x * x, axis=-1, keepdims=True) + RMS_EPS) * g


def _sigmoid(x):
    return 0.5 + 0.5 * jnp.tanh(0.5 * x)


def _ssm_prep_kernel(lr_ref, li_ref, ldt_ref, br_ref, bi_ref, ctr_ref, cti_ref, *refs):
    n_w = len(BIG_WEIGHTS)
    first_f32 = refs[:n_w]
    ar_ref, ai_ref, wb_ref, wc_ref = refs[n_w:n_w + 4]
    first_bf16 = refs[n_w + 4:]
    for src, dst in zip(first_f32, first_bf16):
        dst[...] = src[...].astype(BF16)

    lr = lr_ref[...]
    li = li_ref[...]
    dt = jnp.exp(ldt_ref[...])
    mag = jnp.exp(lr * dt)
    abar_r = mag * jnp.cos(li * dt)
    abar_i = mag * jnp.sin(li * dt)
    den = lr * lr + li * li
    nr = abar_r - 1.0
    ni = abar_i
    coef_r = (nr * lr + ni * li) / den
    coef_i = (ni * lr - nr * li) / den
    br = br_ref[...]
    bi = bi_ref[...]
    ar_ref[...] = abar_r
    ai_ref[...] = abar_i
    bbar_r = coef_r * br - coef_i * bi
    bbar_i = coef_r * bi + coef_i * br

    def iota(shape, axis):
        return lax.broadcasted_iota(jnp.int32, shape, axis)

    tile_in = (iota((SSM_STATE, STATE_PER_HALF), 1) % SSM_STATE
               == iota((SSM_STATE, STATE_PER_HALF), 0)).astype(BF16)
    same_group_in = (iota((CH_PER_HALF, STATE_PER_HALF), 0) // SSM_GROUP
                     == iota((CH_PER_HALF, STATE_PER_HALF), 1) // SSM_STATE)
    for part, bbar in enumerate((bbar_r, bbar_i)):
        tiled = jnp.dot(bbar.astype(BF16), tile_in, preferred_element_type=F32)
        wb_ref[:, part * STATE_PER_HALF:(part + 1) * STATE_PER_HALF] = (
            jnp.where(same_group_in, tiled, 0.0).astype(BF16))

    tile_out = (iota((STATE_PER_HALF, SSM_STATE), 0) % SSM_STATE
                == iota((STATE_PER_HALF, SSM_STATE), 1)).astype(BF16)
    same_group_out = (iota((STATE_PER_HALF, CH_PER_HALF), 0) // SSM_STATE
                      == iota((STATE_PER_HALF, CH_PER_HALF), 1) // SSM_GROUP)
    for part, ct in enumerate((ctr_ref[...], -cti_ref[...])):
        tiled = jnp.dot(tile_out, ct.astype(BF16), preferred_element_type=F32)
        wc_ref[part * STATE_PER_HALF:(part + 1) * STATE_PER_HALF, :] = (
            jnp.where(same_group_out, tiled, 0.0).astype(BF16))


def _ssm_prep(lam_re, lam_im, log_dt, b_re, b_im, c_re, c_im, weights_f32):
    n_half = DEPTH * SSM_HALVES

    def per_channel(a):
        a = jnp.broadcast_to(a[:, :, None, :], (DEPTH, N_SSM_GROUPS, SSM_GROUP, SSM_STATE))
        return a.reshape(n_half, CH_PER_HALF, SSM_STATE)

    def channel_rows(a):
        return jnp.transpose(a, (0, 1, 3, 2)).reshape(n_half, CH_PER_HALF, SSM_STATE)

    def state_rows(a):
        return jnp.transpose(a.reshape(n_half, CH_PER_HALF, SSM_STATE), (0, 2, 1))

    ins = (per_channel(lam_re), per_channel(lam_im),
           per_channel(jnp.broadcast_to(log_dt[..., None], lam_re.shape)),
           channel_rows(b_re), channel_rows(b_im), state_rows(c_re), state_rows(c_im))

    def spec(a):
        return pl.BlockSpec((None,) + a.shape[1:], lambda i: (i, 0, 0))

    outs = (jax.ShapeDtypeStruct((n_half, CH_PER_HALF, SSM_STATE), F32),
            jax.ShapeDtypeStruct((n_half, CH_PER_HALF, SSM_STATE), F32),
            jax.ShapeDtypeStruct((n_half, CH_PER_HALF, RE_IM * STATE_PER_HALF), BF16),
            jax.ShapeDtypeStruct((n_half, RE_IM * STATE_PER_HALF, CH_PER_HALF), BF16))
    in_specs = [spec(a) for a in ins]
    out_specs = [spec(o) for o in outs]
    out_shape = list(outs)
    for w in weights_f32:
        rows, cols = w.shape[1:]
        block_rows = rows // n_half
        assert rows % n_half == 0 and block_rows % 16 == 0
        in_specs.append(pl.BlockSpec((None, block_rows, cols), lambda i: (0, i, 0)))
        out_specs.append(pl.BlockSpec((block_rows, cols), lambda i: (i, 0)))
        out_shape.append(jax.ShapeDtypeStruct((rows, cols), BF16))
    res = pl.pallas_call(
        _ssm_prep_kernel,
        grid=(n_half,),
        in_specs=in_specs,
        out_specs=out_specs,
        out_shape=out_shape,
        compiler_params=pltpu.CompilerParams(
            dimension_semantics=("parallel",), vmem_limit_bytes=V7X_VMEM_LIMIT_BYTES),
        name="ssm_prep",
    )(*ins, *weights_f32)
    return res[:4], res[4:]


def _pool_mixer(pbuf_ref, sb, time0, wpool_ref, pscale_ref):
    row = lax.broadcasted_iota(jnp.int32, (BLOCK_ROWS, 1), 0)
    n_pos = time0 + row // BATCH + 1
    y_pool = []
    for gi, w in enumerate(POOL_WINDOWS):
        x = pbuf_ref[sb, :, gi * POOL_GROUP:(gi + 1) * POOL_GROUP]
        s = x
        span = 1
        while span < w:
            s = s + pltpu.roll(s, span * BATCH, 0)
            span *= 2
        mean = s[POOL_HIST_ROWS:, :] * (1.0 / jnp.minimum(n_pos, w).astype(F32))
        d = mean - x[POOL_HIST_ROWS:, :]
        y_pool.append(jnp.dot(d.astype(BF16), wpool_ref[gi * POOL_GROUP:(gi + 1) * POOL_GROUP, :],
                              preferred_element_type=F32))
    return jnp.concatenate(y_pool, axis=1) * pscale_ref[...]


def _swap_time_pairs(x):
    n, c = x.shape
    tiles = x.reshape(n // V7X_SUBLANES, V7X_SUBLANES, c)
    return pltpu.roll(tiles, BATCH, 1).reshape(n, c)


def _scan_block(bu_ref, sb, j, w, a_r, a_i, lo):
    for m in range(BLOCK_ROWS // V7X_SUBLANES):
        rows = slice(m * V7X_SUBLANES, (m + 1) * V7X_SUBLANES)
        re = bu_ref[sb, j, rows, 0:STATE_PER_HALF]
        im = bu_ref[sb, j, rows, STATE_PER_HALF:]
        x0 = jnp.where(lo, re, im)
        x1 = jnp.where(lo, im, re)
        v0 = a_r * pltpu.roll(w, BATCH, 0) + a_i * w + x0
        w = a_r * pltpu.roll(v0, BATCH, 0) - a_i * v0 + x1
        top = slice(m * V7X_SUBLANES, m * V7X_SUBLANES + BATCH)
        bottom = slice(m * V7X_SUBLANES + BATCH, (m + 1) * V7X_SUBLANES)
        bu_ref[sb, j, top, 0:STATE_PER_HALF] = v0[0:BATCH]
        bu_ref[sb, j, bottom, 0:STATE_PER_HALF] = w[BATCH:]
        bu_ref[sb, j, top, STATE_PER_HALF:] = w[0:BATCH]
        bu_ref[sb, j, bottom, STATE_PER_HALF:] = v0[BATCH:]
    return w


def _ffn_block(h, gffn_ref, wg_ref, wu_ref, wd_ref):
    rstd = lax.rsqrt(jnp.mean(h * h, axis=-1, keepdims=True) + RMS_EPS)
    half_rstd = 0.5 * rstd
    hg = (h * gffn_ref[...]).astype(BF16)
    out = h
    for lo_col, hi_col in zip(FF_CHUNK_EDGES[:-1], FF_CHUNK_EDGES[1:]):
        g_half = jnp.dot(hg, wg_ref[:, lo_col:hi_col], preferred_element_type=F32) * half_rstd
        u = jnp.dot(hg, wu_ref[:, lo_col:hi_col], preferred_element_type=F32) * rstd
        a = ((g_half + g_half * jnp.tanh(g_half)) * u).astype(BF16)
        out = out + jnp.dot(a, wd_ref[lo_col:hi_col, :], preferred_element_type=F32)
    return out


def _layer_kernel(*refs, first, final):
    refs = list(refs)
    h_ref = refs.pop(0)
    (gmix_ref, pscale_ref, wb_ref, ar_ref, ai_ref, wc_ref, dskip_ref, bglu_ref,
     gffn_ref) = [refs.pop(0) for _ in range(N_SMALL_PARAMS)]
    win_ref, wpool_ref, wglu_ref, wout_ref, wg_ref, wu_ref, wd_ref = [
        refs.pop(0) for _ in BIG_WEIGHTS]
    gfin_ref = refs.pop(0)
    next_f32 = [] if final else [refs.pop(0) for _ in BIG_WEIGHTS]
    o_ref = refs.pop(0)
    next_bf16 = [] if final else [refs.pop(0) for _ in BIG_WEIGHTS]
    hist_ref, state_ref, pbuf_ref, bu_ref = [refs.pop(0) for _ in range(4)]
    reorder_ref = refs
    step = pl.program_id(0)

    @pl.when(step == 0)
    def _():
        hist_ref[...] = jnp.zeros_like(hist_ref)
        state_ref[...] = jnp.zeros_like(state_ref)

    lo = lax.broadcasted_iota(jnp.int32, (V7X_SUBLANES, STATE_PER_HALF), 0) < BATCH
    v = [state_ref[j] for j in range(SSM_HALVES)]
    hist = hist_ref[...]
    inproj = []
    for sb in range(SUB_BLOCKS):
        rows = slice(sb * BLOCK_ROWS, (sb + 1) * BLOCK_ROWS)
        if first:
            hbuf_ref = reorder_ref[0]
            t0 = sb * TIME_BLOCK
            for k in range(LANE_SLABS):
                for b in range(BATCH):
                    hbuf_ref[k, pl.ds(b, TIME_BLOCK, stride=BATCH), :] = (
                        h_ref[b, t0:t0 + TIME_BLOCK, k * V7X_LANES:(k + 1) * V7X_LANES])
            h = jnp.concatenate([hbuf_ref[k] for k in range(LANE_SLABS)], axis=1)
        else:
            h = h_ref[rows, :]
        rstd = lax.rsqrt(jnp.mean(h * h, axis=-1, keepdims=True) + RMS_EPS)
        u = jnp.dot((h * gmix_ref[...]).astype(BF16), win_ref[...],
                    preferred_element_type=F32) * rstd
        inproj.append((h, u))

    heads = []
    for sb in range(SUB_BLOCKS):
        h, u = inproj[sb]
        us = u[:, D_POOL:]
        usb = us.astype(BF16)
        usb_swapped = _swap_time_pairs(us).astype(BF16)
        for j in range(SSM_HALVES):
            cols = slice(j * CH_PER_HALF, (j + 1) * CH_PER_HALF)
            bu_ref[sb, j, :, 0:STATE_PER_HALF] = jnp.dot(
                usb[:, cols], wb_ref[j, :, 0:STATE_PER_HALF], preferred_element_type=F32)
            bu_ref[sb, j, :, STATE_PER_HALF:] = jnp.dot(
                usb_swapped[:, cols], wb_ref[j, :, STATE_PER_HALF:], preferred_element_type=F32)

        up = u[:, :D_POOL]
        pbuf_ref[sb, 0:POOL_HIST_ROWS, :] = hist
        pbuf_ref[sb, POOL_HIST_ROWS:, :] = up
        hist = up[BLOCK_ROWS - POOL_HIST_ROWS:, :]
        y_pool = _pool_mixer(pbuf_ref, sb, step * STEP_TIME + sb * TIME_BLOCK, wpool_ref, pscale_ref)
        heads.append((h, us, y_pool))

    for sb in range(SUB_BLOCKS):
        for j in range(SSM_HALVES):
            v[j] = _scan_block(bu_ref, sb, j, v[j], ar_ref[j], ai_ref[j], lo)

    for sb in range(SUB_BLOCKS):
        rows = slice(sb * BLOCK_ROWS, (sb + 1) * BLOCK_ROWS)
        h, us, y_pool = heads[sb]
        y_halves = []
        for j in range(SSM_HALVES):
            y_re = jnp.dot(bu_ref[sb, j, :, 0:STATE_PER_HALF].astype(BF16),
                           wc_ref[j, 0:STATE_PER_HALF, :], preferred_element_type=F32)
            y_im = jnp.dot(bu_ref[sb, j, :, STATE_PER_HALF:].astype(BF16),
                           wc_ref[j, STATE_PER_HALF:, :], preferred_element_type=F32)
            y_halves.append(y_re + _swap_time_pairs(y_im))
        y = jnp.concatenate(y_halves, axis=1) + dskip_ref[...] * us
        y_half = 0.5 * y
        y = y_half + y_half * jnp.tanh(y * (GELU_C + (GELU_C * 0.044715) * (y * y)))
        z = jnp.dot(y.astype(BF16), wglu_ref[...], preferred_element_type=F32) + bglu_ref[...]
        y_ssm = y * _sigmoid(z)

        cat = jnp.concatenate([y_pool, y_ssm], axis=1).astype(BF16)
        h = h + jnp.dot(cat, wout_ref[...], preferred_element_type=F32)

        out = _ffn_block(h, gffn_ref, wg_ref, wu_ref, wd_ref)
        if final:
            obuf_ref = reorder_ref[-1]
            out = _rmsnorm(out, gfin_ref[...])
            t0 = sb * TIME_BLOCK
            for k in range(LANE_SLABS):
                obuf_ref[k] = out[:, k * V7X_LANES:(k + 1) * V7X_LANES]
                for b in range(BATCH):
                    o_ref[b, t0:t0 + TIME_BLOCK, k * V7X_LANES:(k + 1) * V7X_LANES] = (
                        obuf_ref[k, pl.ds(b, TIME_BLOCK, stride=BATCH), :])
        else:
            o_ref[rows, :] = out
    hist_ref[...] = hist
    for j in range(SSM_HALVES):
        state_ref[j] = v[j]

    for src, dst in zip(next_f32, next_bf16):
        dst[...] = src[...].astype(BF16)


def _layer_spec(a, layer):
    tail = (0,) * (a.ndim - 1)
    return pl.BlockSpec((None,) + a.shape[1:], lambda i: (layer,) + tail,
                        pipeline_mode=pl.Buffered(1))


def _resident_spec(a):
    zeros = (0,) * a.ndim
    return pl.BlockSpec(a.shape, lambda i: zeros, pipeline_mode=pl.Buffered(1))


def _cast_specs(w, layer):
    rows, cols = w.shape[1:]
    block_rows = CAST_BLOCK_ROWS if rows % N_STEPS or rows // N_STEPS % 16 else rows // N_STEPS
    last = rows // block_rows - 1
    assert rows % block_rows == 0 and last < N_STEPS
    src = pl.BlockSpec((None, block_rows, cols), lambda i: (layer, jnp.minimum(i, last), 0))
    dst = pl.BlockSpec((block_rows, cols), lambda i: (jnp.minimum(i, last), 0))
    return src, dst, jax.ShapeDtypeStruct((rows, cols), BF16)


def _layer(h, layer, small, weights, gfin, weights_f32):
    first = layer == 0
    final = layer == DEPTH - 1
    row_spec = pl.BlockSpec((STEP_ROWS, D_MODEL), lambda i: (i, 0))
    btd_spec = pl.BlockSpec((BATCH, STEP_TIME, D_MODEL), lambda i: (0, i, 0))
    scratch = [
        pltpu.VMEM((POOL_HIST_ROWS, D_POOL), F32),
        pltpu.VMEM((SSM_HALVES, V7X_SUBLANES, STATE_PER_HALF), F32),
        pltpu.VMEM((SUB_BLOCKS, BLOCK_ROWS + POOL_HIST_ROWS, D_POOL), F32),
        pltpu.VMEM((SUB_BLOCKS, SSM_HALVES, BLOCK_ROWS, RE_IM * STATE_PER_HALF), F32),
    ]
    if first:
        scratch.append(pltpu.VMEM((LANE_SLABS, BLOCK_ROWS, V7X_LANES), F32))
    if final:
        scratch.append(pltpu.VMEM((LANE_SLABS, BLOCK_ROWS, V7X_LANES), F32))
    in_specs = ([btd_spec if first else row_spec] + [_layer_spec(c, layer) for c in small]
                + [_resident_spec(w) for w in weights] + [_resident_spec(gfin)])
    operands = [h, *small, *weights, gfin]
    if final:
        out_specs = [btd_spec]
        out_shape = [jax.ShapeDtypeStruct((BATCH, SEQ, D_MODEL), F32)]
    else:
        out_specs = [row_spec]
        out_shape = [jax.ShapeDtypeStruct((SEQ * BATCH, D_MODEL), F32)]
        for w in weights_f32:
            src, dst, shape = _cast_specs(w, layer + 1)
            in_specs.append(src)
            operands.append(w)
            out_specs.append(dst)
            out_shape.append(shape)
    outs = pl.pallas_call(
        functools.partial(_layer_kernel, first=first, final=final),
        grid=(N_STEPS,),
        in_specs=in_specs,
        out_specs=out_specs,
        out_shape=out_shape,
        scratch_shapes=scratch,
        compiler_params=pltpu.CompilerParams(
            dimension_semantics=("arbitrary",), vmem_limit_bytes=V7X_VMEM_LIMIT_BYTES),
        name="layer_%d" % layer,
    )(*operands)
    return outs[0], outs[1:]


def kernel(x, norm_mix, w_in, w_pool, pool_scale, lam_re, lam_im, log_dt, b_re, b_im, c_re, c_im,
           d_skip, w_glu, b_glu, w_out, norm_ffn, w_gate, w_up, w_down, norm_final):
    assert x.shape == (BATCH, SEQ, D_MODEL) and x.dtype == F32

    weights_f32 = (w_in, w_pool.reshape(DEPTH, D_POOL, POOL_GROUP), w_glu, w_out, w_gate, w_up,
                   w_down)
    (abar_r, abar_i, wb, wc), weights = _ssm_prep(lam_re, lam_im, log_dt, b_re, b_im, c_re, c_im,
                                                  weights_f32)
    per_row = (DEPTH, SSM_HALVES, GROUPS_PER_HALF, SSM_GROUP, SSM_STATE)
    abar_r = abar_r.reshape(per_row)[:, :, :, 0, :].reshape(DEPTH, SSM_HALVES, 1, STATE_PER_HALF)
    abar_i = abar_i.reshape(per_row)[:, :, :, 0, :].reshape(DEPTH, SSM_HALVES, 1, STATE_PER_HALF)
    a_r = jnp.broadcast_to(abar_r, (DEPTH, SSM_HALVES, V7X_SUBLANES, STATE_PER_HALF))
    sign = jnp.where(jnp.arange(V7X_SUBLANES) < BATCH, -1.0, 1.0).astype(F32)[None, None, :, None]
    a_i = sign * abar_i
    wb = wb.reshape(DEPTH, SSM_HALVES, CH_PER_HALF, RE_IM * STATE_PER_HALF)
    wc = wc.reshape(DEPTH, SSM_HALVES, RE_IM * STATE_PER_HALF, CH_PER_HALF)

    small = (
        norm_mix.reshape(DEPTH, 1, D_MODEL), pool_scale.reshape(DEPTH, 1, D_POOL), wb, a_r, a_i, wc,
        d_skip.reshape(DEPTH, 1, D_SSM), b_glu.reshape(DEPTH, 1, D_SSM),
        norm_ffn.reshape(DEPTH, 1, D_MODEL))
    assert len(small) == N_SMALL_PARAMS
    gfin = norm_final.reshape(1, D_MODEL)

    h = x
    for layer in range(DEPTH):
        h, weights = _layer(h, layer, small, weights, gfin, weights_f32)
    return h
```

```python
import functools
import math

import jax
import jax.numpy as jnp
from jax import lax
from jax.experimental import pallas as pl
from jax.experimental.pallas import tpu as pltpu

F32 = jnp.float32
BF16 = jnp.bfloat16

D_MODEL = 1024
BATCH = 4
SEQ = 4096
DEPTH = 4
D_POOL = 512
D_SSM = 512
POOL_WINDOWS = (2, 4, 8, 16)
POOL_GROUP = 128
SSM_GROUP = 16
N_SSM_GROUPS = 32
SSM_STATE = 64
D_FF = 2816
RMS_EPS = 1e-6
GELU_C = math.sqrt(2.0 / math.pi)

V7X_SUBLANES = 8
V7X_LANES = 128
V7X_MXU_DIM = 256
V7X_VMEM_LIMIT_BYTES = 60 * 1024 * 1024

SSM_HALVES = D_SSM // V7X_MXU_DIM
GROUPS_PER_HALF = N_SSM_GROUPS // SSM_HALVES
CH_PER_HALF = GROUPS_PER_HALF * SSM_GROUP
STATE_PER_HALF = GROUPS_PER_HALF * SSM_STATE
RE_IM = 2

TIME_BLOCK = 64
BLOCK_ROWS = TIME_BLOCK * BATCH
SUB_BLOCKS = 2
STEP_TIME = SUB_BLOCKS * TIME_BLOCK
STEP_ROWS = SUB_BLOCKS * BLOCK_ROWS
N_STEPS = SEQ // STEP_TIME
POOL_HIST_STEPS = max(POOL_WINDOWS)
POOL_HIST_ROWS = POOL_HIST_STEPS * BATCH
LANE_SLABS = D_MODEL // V7X_LANES
FF_CHUNK_EDGES = (0, 6 * V7X_MXU_DIM, D_FF)
BIG_WEIGHTS = ("w_in", "w_pool", "w_glu", "w_out", "w_gate", "w_up", "w_down")
N_SMALL_PARAMS = 7
GAIN_OF_WEIGHT = {"w_in": 0, "w_gate": 1, "w_up": 1}
N_GAINS = 2
CAST_BLOCK_ROWS = 128

assert BATCH * RE_IM == V7X_SUBLANES, "the scan tile layout needs batch*2 == 8 sublanes"
assert POOL_HIST_ROWS <= BLOCK_ROWS and D_FF % V7X_MXU_DIM == 0


def _rmsnorm(x, g):
    return x * lax.rsqrt(jnp.mean(x * x, axis=-1, keepdims=True) + RMS_EPS) * g


def _sigmoid(x):
    return 0.5 + 0.5 * jnp.tanh(0.5 * x)


def _cast_weights(f32_refs, gain_col_refs, bf16_refs):
    for name, src, dst in zip(BIG_WEIGHTS, f32_refs, bf16_refs):
        w = src[...]
        if name in GAIN_OF_WEIGHT:
            w = w * gain_col_refs[GAIN_OF_WEIGHT[name]][...]
        dst[...] = w.astype(BF16)


def _ssm_prep_kernel(lr_ref, li_ref, ldt_ref, br_ref, bi_ref, ctr_ref, cti_ref, *refs):
    n_w = len(BIG_WEIGHTS)
    first_f32 = refs[:n_w]
    gain_cols = refs[n_w:n_w + N_GAINS]
    ar_ref, ai_ref, wb_ref, wc_ref = refs[n_w + N_GAINS:n_w + N_GAINS + 4]
    first_bf16 = refs[n_w + N_GAINS + 4:]
    _cast_weights(first_f32, gain_cols, first_bf16)

    lr = lr_ref[...]
    li = li_ref[...]
    dt = jnp.exp(ldt_ref[...])
    mag = jnp.exp(lr * dt)
    abar_r = mag * jnp.cos(li * dt)
    abar_i = mag * jnp.sin(li * dt)
    den = lr * lr + li * li
    nr = abar_r - 1.0
    ni = abar_i
    coef_r = (nr * lr + ni * li) / den
    coef_i = (ni * lr - nr * li) / den
    br = br_ref[...]
    bi = bi_ref[...]
    ar_ref[...] = abar_r
    ai_ref[...] = abar_i
    bbar_r = coef_r * br - coef_i * bi
    bbar_i = coef_r * bi + coef_i * br

    def iota(shape, axis):
        return lax.broadcasted_iota(jnp.int32, shape, axis)

    tile_in = (iota((SSM_STATE, STATE_PER_HALF), 1) % SSM_STATE
               == iota((SSM_STATE, STATE_PER_HALF), 0)).astype(BF16)
    same_group_in = (iota((CH_PER_HALF, STATE_PER_HALF), 0) // SSM_GROUP
                     == iota((CH_PER_HALF, STATE_PER_HALF), 1) // SSM_STATE)
    for part, bbar in enumerate((bbar_r, bbar_i)):
        tiled = jnp.dot(bbar.astype(BF16), tile_in, preferred_element_type=F32)
        wb_ref[:, part * STATE_PER_HALF:(part + 1) * STATE_PER_HALF] = (
            jnp.where(same_group_in, tiled, 0.0).astype(BF16))

    tile_out = (iota((STATE_PER_HALF, SSM_STATE), 0) % SSM_STATE
                == iota((STATE_PER_HALF, SSM_STATE), 1)).astype(BF16)
    same_group_out = (iota((STATE_PER_HALF, CH_PER_HALF), 0) // SSM_STATE
                      == iota((STATE_PER_HALF, CH_PER_HALF), 1) // SSM_GROUP)
    for part, ct in enumerate((ctr_ref[...], -cti_ref[...])):
        tiled = jnp.dot(tile_out, ct.astype(BF16), preferred_element_type=F32)
        wc_ref[part * STATE_PER_HALF:(part + 1) * STATE_PER_HALF, :] = (
            jnp.where(same_group_out, tiled, 0.0).astype(BF16))


def _ssm_prep(lam_re, lam_im, log_dt, b_re, b_im, c_re, c_im, weights_f32, gain_cols):
    n_half = DEPTH * SSM_HALVES

    def per_channel(a):
        a = jnp.broadcast_to(a[:, :, None, :], (DEPTH, N_SSM_GROUPS, SSM_GROUP, SSM_STATE))
        return a.reshape(n_half, CH_PER_HALF, SSM_STATE)

    def channel_rows(a):
        return jnp.transpose(a, (0, 1, 3, 2)).reshape(n_half, CH_PER_HALF, SSM_STATE)

    def state_rows(a):
        return jnp.transpose(a.reshape(n_half, CH_PER_HALF, SSM_STATE), (0, 2, 1))

    ins = (per_channel(lam_re), per_channel(lam_im),
           per_channel(jnp.broadcast_to(log_dt[..., None], lam_re.shape)),
           channel_rows(b_re), channel_rows(b_im), state_rows(c_re), state_rows(c_im))

    def spec(a):
        return pl.BlockSpec((None,) + a.shape[1:], lambda i: (i, 0, 0))

    outs = (jax.ShapeDtypeStruct((n_half, CH_PER_HALF, SSM_STATE), F32),
            jax.ShapeDtypeStruct((n_half, CH_PER_HALF, SSM_STATE), F32),
            jax.ShapeDtypeStruct((n_half, CH_PER_HALF, RE_IM * STATE_PER_HALF), BF16),
            jax.ShapeDtypeStruct((n_half, RE_IM * STATE_PER_HALF, CH_PER_HALF), BF16))
    in_specs = [spec(a) for a in ins]
    out_specs = [spec(o) for o in outs]
    out_shape = list(outs)
    for w in weights_f32:
        rows, cols = w.shape[1:]
        block_rows = rows // n_half
        assert rows % n_half == 0 and block_rows % 16 == 0
        in_specs.append(pl.BlockSpec((None, block_rows, cols), lambda i: (0, i, 0)))
        out_specs.append(pl.BlockSpec((block_rows, cols), lambda i: (i, 0)))
        out_shape.append(jax.ShapeDtypeStruct((rows, cols), BF16))
    for g in gain_cols:
        in_specs.append(pl.BlockSpec((None, D_MODEL // n_half, 1), lambda i: (0, i, 0)))
    res = pl.pallas_call(
        _ssm_prep_kernel,
        grid=(n_half,),
        in_specs=in_specs,
        out_specs=out_specs,
        out_shape=out_shape,
        compiler_params=pltpu.CompilerParams(
            dimension_semantics=("parallel",), vmem_limit_bytes=V7X_VMEM_LIMIT_BYTES),
        name="ssm_prep",
    )(*ins, *weights_f32, *gain_cols)
    return res[:4], res[4:]


def _pool_mixer(pbuf_ref, sb, time0, wpool_ref, pscale_ref):
    row = lax.broadcasted_iota(jnp.int32, (BLOCK_ROWS, 1), 0)
    n_pos = time0 + row // BATCH + 1
    y_pool = []
    for gi, w in enumerate(POOL_WINDOWS):
        x = pbuf_ref[sb, :, gi * POOL_GROUP:(gi + 1) * POOL_GROUP]
        s = x
        span = 1
        while span < w:
            s = s + pltpu.roll(s, span * BATCH, 0)
            span *= 2
        mean = s[POOL_HIST_ROWS:, :] * (1.0 / jnp.minimum(n_pos, w).astype(F32))
        d = mean - x[POOL_HIST_ROWS:, :]
        y_pool.append(jnp.dot(d.astype(BF16), wpool_ref[gi * POOL_GROUP:(gi + 1) * POOL_GROUP, :],
                              preferred_element_type=F32))
    return jnp.concatenate(y_pool, axis=1) * pscale_ref[...]


def _swap_time_pairs(x):
    n, c = x.shape
    tiles = x.reshape(n // V7X_SUBLANES, V7X_SUBLANES, c)
    return pltpu.roll(tiles, BATCH, 1).reshape(n, c)


def _scan_block(bu_ref, sb, j, w, a_r, a_i, lo):
    for m in range(BLOCK_ROWS // V7X_SUBLANES):
        rows = slice(m * V7X_SUBLANES, (m + 1) * V7X_SUBLANES)
        re = bu_ref[sb, j, rows, 0:STATE_PER_HALF]
        im = bu_ref[sb, j, rows, STATE_PER_HALF:]
        x0 = jnp.where(lo, re, im)
        x1 = jnp.where(lo, im, re)
        v0 = a_r * pltpu.roll(w, BATCH, 0) + a_i * w + x0
        w = a_r * pltpu.roll(v0, BATCH, 0) - a_i * v0 + x1
        top = slice(m * V7X_SUBLANES, m * V7X_SUBLANES + BATCH)
        bottom = slice(m * V7X_SUBLANES + BATCH, (m + 1) * V7X_SUBLANES)
        bu_ref[sb, j, top, 0:STATE_PER_HALF] = v0[0:BATCH]
        bu_ref[sb, j, bottom, 0:STATE_PER_HALF] = w[BATCH:]
        bu_ref[sb, j, top, STATE_PER_HALF:] = w[0:BATCH]
        bu_ref[sb, j, bottom, STATE_PER_HALF:] = v0[BATCH:]
    return w


def _ffn_block(h, wg_ref, wu_ref, wd_ref):
    rstd = lax.rsqrt(jnp.mean(h * h, axis=-1, keepdims=True) + RMS_EPS)
    half_rstd = 0.5 * rstd
    hg = h.astype(BF16)
    out = h
    for lo_col, hi_col in zip(FF_CHUNK_EDGES[:-1], FF_CHUNK_EDGES[1:]):
        g_half = jnp.dot(hg, wg_ref[:, lo_col:hi_col], preferred_element_type=F32) * half_rstd
        u = jnp.dot(hg, wu_ref[:, lo_col:hi_col], preferred_element_type=F32) * rstd
        a = ((g_half + g_half * jnp.tanh(g_half)) * u).astype(BF16)
        out = out + jnp.dot(a, wd_ref[lo_col:hi_col, :], preferred_element_type=F32)
    return out


def _layer_kernel(*refs, first, final):
    refs = list(refs)
    h_ref = refs.pop(0)
    (pscale_ref, wb_ref, ar_ref, ai_ref, wc_ref, dskip_ref,
     bglu_ref) = [refs.pop(0) for _ in range(N_SMALL_PARAMS)]
    win_ref, wpool_ref, wglu_ref, wout_ref, wg_ref, wu_ref, wd_ref = [
        refs.pop(0) for _ in BIG_WEIGHTS]
    gfin_ref = refs.pop(0)
    next_f32 = [] if final else [refs.pop(0) for _ in BIG_WEIGHTS]
    next_gain_cols = [] if final else [refs.pop(0) for _ in range(N_GAINS)]
    o_ref = refs.pop(0)
    next_bf16 = [] if final else [refs.pop(0) for _ in BIG_WEIGHTS]
    hist_ref, state_ref, pbuf_ref, bu_ref = [refs.pop(0) for _ in range(4)]
    reorder_ref = refs
    step = pl.program_id(0)

    @pl.when(step == 0)
    def _():
        hist_ref[...] = jnp.zeros_like(hist_ref)
        state_ref[...] = jnp.zeros_like(state_ref)

    lo = lax.broadcasted_iota(jnp.int32, (V7X_SUBLANES, STATE_PER_HALF), 0) < BATCH
    v = [state_ref[j] for j in range(SSM_HALVES)]
    hist = hist_ref[...]
    inproj = []
    for sb in range(SUB_BLOCKS):
        rows = slice(sb * BLOCK_ROWS, (sb + 1) * BLOCK_ROWS)
        if first:
            hbuf_ref = reorder_ref[0]
            t0 = sb * TIME_BLOCK
            for k in range(LANE_SLABS):
                for b in range(BATCH):
                    hbuf_ref[k, pl.ds(b, TIME_BLOCK, stride=BATCH), :] = (
                        h_ref[b, t0:t0 + TIME_BLOCK, k * V7X_LANES:(k + 1) * V7X_LANES])
            h = jnp.concatenate([hbuf_ref[k] for k in range(LANE_SLABS)], axis=1)
        else:
            h = h_ref[rows, :]
        rstd = lax.rsqrt(jnp.mean(h * h, axis=-1, keepdims=True) + RMS_EPS)
        u = jnp.dot(h.astype(BF16), win_ref[...], preferred_element_type=F32) * rstd
        inproj.append((h, u))

    heads = []
    for sb in range(SUB_BLOCKS):
        h, u = inproj[sb]
        us = u[:, D_POOL:]
        usb = us.astype(BF16)
        usb_swapped = _swap_time_pairs(us).astype(BF16)
        for j in range(SSM_HALVES):
            cols = slice(j * CH_PER_HALF, (j + 1) * CH_PER_HALF)
            bu_ref[sb, j, :, 0:STATE_PER_HALF] = jnp.dot(
                usb[:, cols], wb_ref[j, :, 0:STATE_PER_HALF], preferred_element_type=F32)
            bu_ref[sb, j, :, STATE_PER_HALF:] = jnp.dot(
                usb_swapped[:, cols], wb_ref[j, :, STATE_PER_HALF:], preferred_element_type=F32)

        up = u[:, :D_POOL]
        pbuf_ref[sb, 0:POOL_HIST_ROWS, :] = hist
        pbuf_ref[sb, POOL_HIST_ROWS:, :] = up
        hist = up[BLOCK_ROWS - POOL_HIST_ROWS:, :]
        y_pool = _pool_mixer(pbuf_ref, sb, step * STEP_TIME + sb * TIME_BLOCK, wpool_ref, pscale_ref)
        heads.append((h, us, y_pool))

    for sb in range(SUB_BLOCKS):
        for j in range(SSM_HALVES):
            v[j] = _scan_block(bu_ref, sb, j, v[j], ar_ref[j], ai_ref[j], lo)

    for sb in range(SUB_BLOCKS):
        rows = slice(sb * BLOCK_ROWS, (sb + 1) * BLOCK_ROWS)
        h, us, y_pool = heads[sb]
        y_halves = []
        for j in range(SSM_HALVES):
            y_re = jnp.dot(bu_ref[sb, j, :, 0:STATE_PER_HALF].astype(BF16),
                           wc_ref[j, 0:STATE_PER_HALF, :], preferred_element_type=F32)
            y_im = jnp.dot(bu_ref[sb, j, :, STATE_PER_HALF:].astype(BF16),
                           wc_ref[j, STATE_PER_HALF:, :], preferred_element_type=F32)
            y_halves.append(y_re + _swap_time_pairs(y_im))
        y = jnp.concatenate(y_halves, axis=1) + dskip_ref[...] * us
        y_half = 0.5 * y
        y = y_half + y_half * jnp.tanh(y * (GELU_C + (GELU_C * 0.044715) * (y * y)))
        z = jnp.dot(y.astype(BF16), wglu_ref[...], preferred_element_type=F32) + bglu_ref[...]
        y_ssm = y * _sigmoid(z)

        cat = jnp.concatenate([y_pool, y_ssm], axis=1).astype(BF16)
        h = h + jnp.dot(cat, wout_ref[...], preferred_element_type=F32)

        out = _ffn_block(h, wg_ref, wu_ref, wd_ref)
        if final:
            obuf_ref = reorder_ref[-1]
            out = _rmsnorm(out, gfin_ref[...])
            t0 = sb * TIME_BLOCK
            for k in range(LANE_SLABS):
                obuf_ref[k] = out[:, k * V7X_LANES:(k + 1) * V7X_LANES]
                for b in range(BATCH):
                    o_ref[b, t0:t0 + TIME_BLOCK, k * V7X_LANES:(k + 1) * V7X_LANES] = (
                        obuf_ref[k, pl.ds(b, TIME_BLOCK, stride=BATCH), :])
        else:
            o_ref[rows, :] = out
    hist_ref[...] = hist
    for j in range(SSM_HALVES):
        state_ref[j] = v[j]

    _cast_weights(next_f32, next_gain_cols, next_bf16)


def _layer_spec(a, layer):
    tail = (0,) * (a.ndim - 1)
    return pl.BlockSpec((None,) + a.shape[1:], lambda i: (layer,) + tail,
                        pipeline_mode=pl.Buffered(1))


def _resident_spec(a):
    zeros = (0,) * a.ndim
    return pl.BlockSpec(a.shape, lambda i: zeros, pipeline_mode=pl.Buffered(1))


def _cast_specs(w, layer):
    rows, cols = w.shape[1:]
    block_rows = CAST_BLOCK_ROWS if rows % N_STEPS or rows // N_STEPS % 16 else rows // N_STEPS
    last = rows // block_rows - 1
    assert rows % block_rows == 0 and last < N_STEPS
    src = pl.BlockSpec((None, block_rows, cols), lambda i: (layer, jnp.minimum(i, last), 0))
    dst = pl.BlockSpec((block_rows, cols), lambda i: (jnp.minimum(i, last), 0))
    return src, dst, jax.ShapeDtypeStruct((rows, cols), BF16)


def _layer(h, layer, small, weights, gfin, weights_f32, gain_cols):
    first = layer == 0
    final = layer == DEPTH - 1
    row_spec = pl.BlockSpec((STEP_ROWS, D_MODEL), lambda i: (i, 0))
    btd_spec = pl.BlockSpec((BATCH, STEP_TIME, D_MODEL), lambda i: (0, i, 0))
    scratch = [
        pltpu.VMEM((POOL_HIST_ROWS, D_POOL), F32),
        pltpu.VMEM((SSM_HALVES, V7X_SUBLANES, STATE_PER_HALF), F32),
        pltpu.VMEM((SUB_BLOCKS, BLOCK_ROWS + POOL_HIST_ROWS, D_POOL), F32),
        pltpu.VMEM((SUB_BLOCKS, SSM_HALVES, BLOCK_ROWS, RE_IM * STATE_PER_HALF), F32),
    ]
    if first:
        scratch.append(pltpu.VMEM((LANE_SLABS, BLOCK_ROWS, V7X_LANES), F32))
    if final:
        scratch.append(pltpu.VMEM((LANE_SLABS, BLOCK_ROWS, V7X_LANES), F32))
    in_specs = ([btd_spec if first else row_spec] + [_layer_spec(c, layer) for c in small]
                + [_resident_spec(w) for w in weights] + [_resident_spec(gfin)])
    operands = [h, *small, *weights, gfin]
    if final:
        out_specs = [btd_spec]
        out_shape = [jax.ShapeDtypeStruct((BATCH, SEQ, D_MODEL), F32)]
    else:
        out_specs = [row_spec]
        out_shape = [jax.ShapeDtypeStruct((SEQ * BATCH, D_MODEL), F32)]
        for w in weights_f32:
            src, dst, shape = _cast_specs(w, layer + 1)
            in_specs.append(src)
            operands.append(w)
            out_specs.append(dst)
            out_shape.append(shape)
        for g in gain_cols:
            in_specs.append(pl.BlockSpec((None, D_MODEL // N_STEPS, 1),
                                         lambda i: (layer + 1, i, 0)))
            operands.append(g)
    outs = pl.pallas_call(
        functools.partial(_layer_kernel, first=first, final=final),
        grid=(N_STEPS,),
        in_specs=in_specs,
        out_specs=out_specs,
        out_shape=out_shape,
        scratch_shapes=scratch,
        compiler_params=pltpu.CompilerParams(
            dimension_semantics=("arbitrary",), vmem_limit_bytes=V7X_VMEM_LIMIT_BYTES),
        name="layer_%d" % layer,
    )(*operands)
    return outs[0], outs[1:]


def kernel(x, norm_mix, w_in, w_pool, pool_scale, lam_re, lam_im, log_dt, b_re, b_im, c_re, c_im,
           d_skip, w_glu, b_glu, w_out, norm_ffn, w_gate, w_up, w_down, norm_final):
    assert x.shape == (BATCH, SEQ, D_MODEL) and x.dtype == F32

    weights_f32 = (w_in, w_pool.reshape(DEPTH, D_POOL, POOL_GROUP), w_glu, w_out, w_gate, w_up,
                   w_down)
    gain_cols = (norm_mix.reshape(DEPTH, D_MODEL, 1), norm_ffn.reshape(DEPTH, D_MODEL, 1))
    (abar_r, abar_i, wb, wc), weights = _ssm_prep(lam_re, lam_im, log_dt, b_re, b_im, c_re, c_im,
                                                  weights_f32, gain_cols)
    per_row = (DEPTH, SSM_HALVES, GROUPS_PER_HALF, SSM_GROUP, SSM_STATE)
    abar_r = abar_r.reshape(per_row)[:, :, :, 0, :].reshape(DEPTH, SSM_HALVES, 1, STATE_PER_HALF)
    abar_i = abar_i.reshape(per_row)[:, :, :, 0, :].reshape(DEPTH, SSM_HALVES, 1, STATE_PER_HALF)
    a_r = jnp.broadcast_to(abar_r, (DEPTH, SSM_HALVES, V7X_SUBLANES, STATE_PER_HALF))
    sign = jnp.where(jnp.arange(V7X_SUBLANES) < BATCH, -1.0, 1.0).astype(F32)[None, None, :, None]
    a_i = sign * abar_i
    wb = wb.reshape(DEPTH, SSM_HALVES, CH_PER_HALF, RE_IM * STATE_PER_HALF)
    wc = wc.reshape(DEPTH, SSM_HALVES, RE_IM * STATE_PER_HALF, CH_PER_HALF)

    small = (pool_scale.reshape(DEPTH, 1, D_POOL), wb, a_r, a_i, wc,
             d_skip.reshape(DEPTH, 1, D_SSM), b_glu.reshape(DEPTH, 1, D_SSM))
    assert len(small) == N_SMALL_PARAMS
    gfin = norm_final.reshape(1, D_MODEL)

    h = x
    for layer in range(DEPTH):
        h, weights = _layer(h, layer, small, weights, gfin, weights_f32, gain_cols)
    return h
```

```python
import functools
import math

import jax
import jax.numpy as jnp
from jax import lax
from jax.experimental import pallas as pl
from jax.experimental.pallas import tpu as pltpu

F32 = jnp.float32
BF16 = jnp.bfloat16

D_MODEL = 1024
BATCH = 4
SEQ = 4096
DEPTH = 4
D_POOL = 512
D_SSM = 512
POOL_WINDOWS = (2, 4, 8, 16)
POOL_GROUP = 128
SSM_GROUP = 16
N_SSM_GROUPS = 32
SSM_STATE = 64
D_FF = 2816
RMS_EPS = 1e-6
GELU_C = math.sqrt(2.0 / math.pi)

V7X_SUBLANES = 8
V7X_LANES = 128
V7X_MXU_DIM = 256
V7X_VMEM_LIMIT_BYTES = 60 * 1024 * 1024

SSM_HALVES = D_SSM // V7X_MXU_DIM
GROUPS_PER_HALF = N_SSM_GROUPS // SSM_HALVES
CH_PER_HALF = GROUPS_PER_HALF * SSM_GROUP
STATE_PER_HALF = GROUPS_PER_HALF * SSM_STATE
RE_IM = 2

TIME_BLOCK = 64
BLOCK_ROWS = TIME_BLOCK * BATCH
SUB_BLOCKS = 2
STEP_TIME = SUB_BLOCKS * TIME_BLOCK
STEP_ROWS = SUB_BLOCKS * BLOCK_ROWS
N_STEPS = SEQ // STEP_TIME
POOL_HIST_STEPS = max(POOL_WINDOWS)
POOL_HIST_ROWS = POOL_HIST_STEPS * BATCH
LANE_SLABS = D_MODEL // V7X_LANES
FF_CHUNK_EDGES = (0, 6 * V7X_MXU_DIM, D_FF)
BIG_WEIGHTS = ("w_in", "w_pool", "w_glu", "w_out", "w_gate", "w_up", "w_down")
N_SMALL_PARAMS = 9
CAST_BLOCK_ROWS = 128

assert BATCH * RE_IM == V7X_SUBLANES, "the scan tile layout needs batch*2 == 8 sublanes"
assert POOL_HIST_ROWS <= BLOCK_ROWS and D_FF % V7X_MXU_DIM == 0


def _rmsnorm(x, g):
    return x * lax.rsqrt(jnp.mean(x * x, axis=-1, keepdims=True) + RMS_EPS) * g


def _sigmoid(x):
    return 0.5 + 0.5 * jnp.tanh(0.5 * x)


def _ssm_prep_kernel(lr_ref, li_ref, ldt_ref, br_ref, bi_ref, ctr_ref, cti_ref, *refs):
    n_w = len(BIG_WEIGHTS)
    first_f32 = refs[:n_w]
    ar_ref, ai_ref, wb_ref, wc_ref = refs[n_w:n_w + 4]
    first_bf16 = refs[n_w + 4:]
    for src, dst in zip(first_f32, first_bf16):
        dst[...] = src[...].astype(BF16)

    lr = lr_ref[...]
    li = li_ref[...]
    dt = jnp.exp(ldt_ref[...])
    mag = jnp.exp(lr * dt)
    abar_r = mag * jnp.cos(li * dt)
    abar_i = mag * jnp.sin(li * dt)
    den = lr * lr + li * li
    nr = abar_r - 1.0
    ni = abar_i
    coef_r = (nr * lr + ni * li) / den
    coef_i = (ni * lr - nr * li) / den
    br = br_ref[...]
    bi = bi_ref[...]
    ar_ref[...] = abar_r
    ai_ref[...] = abar_i
    bbar_r = coef_r * br - coef_i * bi
    bbar_i = coef_r * bi + coef_i * br

    def iota(shape, axis):
        return lax.broadcasted_iota(jnp.int32, shape, axis)

    tile_in = (iota((SSM_STATE, STATE_PER_HALF), 1) % SSM_STATE
               == iota((SSM_STATE, STATE_PER_HALF), 0)).astype(BF16)
    same_group_in = (iota((CH_PER_HALF, STATE_PER_HALF), 0) // SSM_GROUP
                     == iota((CH_PER_HALF, STATE_PER_HALF), 1) // SSM_STATE)
    for part, bbar in enumerate((bbar_r, bbar_i)):
        tiled = jnp.dot(bbar.astype(BF16), tile_in, preferred_element_type=F32)
        wb_ref[:, part * STATE_PER_HALF:(part + 1) * STATE_PER_HALF] = (
            jnp.where(same_group_in, tiled, 0.0).astype(BF16))

    tile_out = (iota((STATE_PER_HALF, SSM_STATE), 0) % SSM_STATE
                == iota((STATE_PER_HALF, SSM_STATE), 1)).astype(BF16)
    same_group_out = (iota((STATE_PER_HALF, CH_PER_HALF), 0) // SSM_STATE
                      == iota((STATE_PER_HALF, CH_PER_HALF), 1) // SSM_GROUP)
    for part, ct in enumerate((ctr_ref[...], -cti_ref[...])):
        tiled = jnp.dot(tile_out, ct.astype(BF16), preferred_element_type=F32)
        wc_ref[part * STATE_PER_HALF:(part + 1) * STATE_PER_HALF, :] = (
            jnp.where(same_group_out, tiled, 0.0).astype(BF16))


def _ssm_prep(lam_re, lam_im, log_dt, b_re, b_im, c_re, c_im, weights_f32):
    n_half = DEPTH * SSM_HALVES

    def per_channel(a):
        a = jnp.broadcast_to(a[:, :, None, :], (DEPTH, N_SSM_GROUPS, SSM_GROUP, SSM_STATE))
        return a.reshape(n_half, CH_PER_HALF, SSM_STATE)

    def channel_rows(a):
        return jnp.transpose(a, (0, 1, 3, 2)).reshape(n_half, CH_PER_HALF, SSM_STATE)

    def state_rows(a):
        return jnp.transpose(a.reshape(n_half, CH_PER_HALF, SSM_STATE), (0, 2, 1))

    ins = (per_channel(lam_re), per_channel(lam_im),
           per_channel(jnp.broadcast_to(log_dt[..., None], lam_re.shape)),
           channel_rows(b_re), channel_rows(b_im), state_rows(c_re), state_rows(c_im))

    def spec(a):
        return pl.BlockSpec((None,) + a.shape[1:], lambda i: (i, 0, 0))

    outs = (jax.ShapeDtypeStruct((n_half, CH_PER_HALF, SSM_STATE), F32),
            jax.ShapeDtypeStruct((n_half, CH_PER_HALF, SSM_STATE), F32),
            jax.ShapeDtypeStruct((n_half, CH_PER_HALF, RE_IM * STATE_PER_HALF), BF16),
            jax.ShapeDtypeStruct((n_half, RE_IM * STATE_PER_HALF, CH_PER_HALF), BF16))
    in_specs = [spec(a) for a in ins]
    out_specs = [spec(o) for o in outs]
    out_shape = list(outs)
    for w in weights_f32:
        rows, cols = w.shape[1:]
        block_rows = rows // n_half
        assert rows % n_half == 0 and block_rows % 16 == 0
        in_specs.append(pl.BlockSpec((None, block_rows, cols), lambda i: (0, i, 0)))
        out_specs.append(pl.BlockSpec((block_rows, cols), lambda i: (i, 0)))
        out_shape.append(jax.ShapeDtypeStruct((rows, cols), BF16))
    res = pl.pallas_call(
        _ssm_prep_kernel,
        grid=(n_half,),
        in_specs=in_specs,
        out_specs=out_specs,
        out_shape=out_shape,
        compiler_params=pltpu.CompilerParams(
            dimension_semantics=("parallel",), vmem_limit_bytes=V7X_VMEM_LIMIT_BYTES),
        name="ssm_prep",
    )(*ins, *weights_f32)
    return res[:4], res[4:]


def _pool_mixer(pbuf_ref, sb, time0, wpool_ref, pscale_ref):
    row = lax.broadcasted_iota(jnp.int32, (BLOCK_ROWS, 1), 0)
    n_pos = time0 + row // BATCH + 1
    y_pool = []
    for gi, w in enumerate(POOL_WINDOWS):
        x = pbuf_ref[sb, :, gi * POOL_GROUP:(gi + 1) * POOL_GROUP]
        s = x
        span = 1
        while span < w:
            s = s + pltpu.roll(s, span * BATCH, 0)
            span *= 2
        mean = s[POOL_HIST_ROWS:, :] * (1.0 / jnp.minimum(n_pos, w).astype(F32))
        d = mean - x[POOL_HIST_ROWS:, :]
        y_pool.append(jnp.dot(d.astype(BF16), wpool_ref[gi * POOL_GROUP:(gi + 1) * POOL_GROUP, :],
                              preferred_element_type=F32))
    return jnp.concatenate(y_pool, axis=1) * pscale_ref[...]


def _swap_time_pairs(x):
    n, c = x.shape
    tiles = x.reshape(n // V7X_SUBLANES, V7X_SUBLANES, c)
    return pltpu.roll(tiles, BATCH, 1).reshape(n, c)


def _scan_block(bu_ref, sb, j, w, a_r, a_i, lo):
    for m in range(BLOCK_ROWS // V7X_SUBLANES):
        rows = slice(m * V7X_SUBLANES, (m + 1) * V7X_SUBLANES)
        re = bu_ref[sb, j, rows, 0:STATE_PER_HALF]
        im = bu_ref[sb, j, rows, STATE_PER_HALF:]
        x0 = jnp.where(lo, re, im)
        x1 = jnp.where(lo, im, re)
        v0 = a_r * pltpu.roll(w, BATCH, 0) + a_i * w + x0
        w = a_r * pltpu.roll(v0, BATCH, 0) - a_i * v0 + x1
        top = slice(m * V7X_SUBLANES, m * V7X_SUBLANES + BATCH)
        bottom = slice(m * V7X_SUBLANES + BATCH, (m + 1) * V7X_SUBLANES)
        bu_ref[sb, j, top, 0:STATE_PER_HALF] = v0[0:BATCH]
        bu_ref[sb, j, bottom, 0:STATE_PER_HALF] = w[BATCH:]
        bu_ref[sb, j, top, STATE_PER_HALF:] = w[0:BATCH]
        bu_ref[sb, j, bottom, STATE_PER_HALF:] = v0[BATCH:]
    return w


def _ffn_block(h, gffn_ref, wg_ref, wu_ref, wd_ref):
    rstd = lax.rsqrt(jnp.mean(h * h, axis=-1, keepdims=True) + RMS_EPS)
    half_rstd = 0.5 * rstd
    hg = (h * gffn_ref[...]).astype(BF16)
    out = h
    for lo_col, hi_col in zip(FF_CHUNK_EDGES[:-1], FF_CHUNK_EDGES[1:]):
        g_half = jnp.dot(hg, wg_ref[:, lo_col:hi_col], preferred_element_type=F32) * half_rstd
        u = jnp.dot(hg, wu_ref[:, lo_col:hi_col], preferred_element_type=F32) * rstd
        a = ((g_half + g_half * jnp.tanh(g_half)) * u).astype(BF16)
        out = out + jnp.dot(a, wd_ref[lo_col:hi_col, :], preferred_element_type=F32)
    return out


def _layer_kernel(*refs, first, final):
    refs = list(refs)
    h_ref = refs.pop(0)
    (gmix_ref, pscale_ref, wb_ref, ar_ref, ai_ref, wc_ref, dskip_ref, bglu_ref,
     gffn_ref) = [refs.pop(0) for _ in range(N_SMALL_PARAMS)]
    win_ref, wpool_ref, wglu_ref, wout_ref, wg_ref, wu_ref, wd_ref = [
        refs.pop(0) for _ in BIG_WEIGHTS]
    gfin_ref = refs.pop(0)
    next_f32 = [] if final else [refs.pop(0) for _ in BIG_WEIGHTS]
    o_ref = refs.pop(0)
    next_bf16 = [] if final else [refs.pop(0) for _ in BIG_WEIGHTS]
    hist_ref, state_ref, pbuf_ref, bu_ref = [refs.pop(0) for _ in range(4)]
    reorder_ref = refs
    step = pl.program_id(0)

    @pl.when(step == 0)
    def _():
        hist_ref[...] = jnp.zeros_like(hist_ref)
        state_ref[...] = jnp.zeros_like(state_ref)

    lo = lax.broadcasted_iota(jnp.int32, (V7X_SUBLANES, STATE_PER_HALF), 0) < BATCH
    v = [state_ref[j] for j in range(SSM_HALVES)]
    hist = hist_ref[...]
    inproj = []
    for sb in range(SUB_BLOCKS):
        rows = slice(sb * BLOCK_ROWS, (sb + 1) * BLOCK_ROWS)
        if first:
            hbuf_ref = reorder_ref[0]
            t0 = sb * TIME_BLOCK
            for k in range(LANE_SLABS):
                for b in range(BATCH):
                    hbuf_ref[k, pl.ds(b, TIME_BLOCK, stride=BATCH), :] = (
                        h_ref[b, t0:t0 + TIME_BLOCK, k * V7X_LANES:(k + 1) * V7X_LANES])
            h = jnp.concatenate([hbuf_ref[k] for k in range(LANE_SLABS)], axis=1)
        else:
            h = h_ref[rows, :]
        rstd = lax.rsqrt(jnp.mean(h * h, axis=-1, keepdims=True) + RMS_EPS)
        u = jnp.dot((h * gmix_ref[...]).astype(BF16), win_ref[...],
                    preferred_element_type=F32) * rstd
        inproj.append((h, u))

    heads = []
    for sb in range(SUB_BLOCKS):
        h, u = inproj[sb]
        us = u[:, D_POOL:]
        usb = us.astype(BF16)
        usb_swapped = _swap_time_pairs(us).astype(BF16)
        for j in range(SSM_HALVES):
            cols = slice(j * CH_PER_HALF, (j + 1) * CH_PER_HALF)
            bu_ref[sb, j, :, 0:STATE_PER_HALF] = jnp.dot(
                usb[:, cols], wb_ref[j, :, 0:STATE_PER_HALF], preferred_element_type=F32)
            bu_ref[sb, j, :, STATE_PER_HALF:] = jnp.dot(
                usb_swapped[:, cols], wb_ref[j, :, STATE_PER_HALF:], preferred_element_type=F32)

        up = u[:, :D_POOL]
        pbuf_ref[sb, 0:POOL_HIST_ROWS, :] = hist
        pbuf_ref[sb, POOL_HIST_ROWS:, :] = up
        hist = up[BLOCK_ROWS - POOL_HIST_ROWS:, :]
        y_pool = _pool_mixer(pbuf_ref, sb, step * STEP_TIME + sb * TIME_BLOCK, wpool_ref, pscale_ref)
        heads.append((h, us, y_pool))

    for sb in range(SUB_BLOCKS):
        for j in range(SSM_HALVES):
            v[j] = _scan_block(bu_ref, sb, j, v[j], ar_ref[j], ai_ref[j], lo)

    for sb in range(SUB_BLOCKS):
        rows = slice(sb * BLOCK_ROWS, (sb + 1) * BLOCK_ROWS)
        h, us, y_pool = heads[sb]
        y_halves = []
        for j in range(SSM_HALVES):
            y_re = jnp.dot(bu_ref[sb, j, :, 0:STATE_PER_HALF].astype(BF16),
                           wc_ref[j, 0:STATE_PER_HALF, :], preferred_element_type=F32)
            y_im = jnp.dot(bu_ref[sb, j, :, STATE_PER_HALF:].astype(BF16),
                           wc_ref[j, STATE_PER_HALF:, :], preferred_element_type=F32)
            y_halves.append(y_re + _swap_time_pairs(y_im))
        y = jnp.concatenate(y_halves, axis=1) + dskip_ref[...] * us
        y_half = 0.5 * y
        y = y_half + y_half * jnp.tanh(y * (GELU_C + (GELU_C * 0.044715) * (y * y)))
        z = jnp.dot(y.astype(BF16), wglu_ref[...], preferred_element_type=F32) + bglu_ref[...]
        y_ssm = y * _sigmoid(z)

        cat = jnp.concatenate([y_pool, y_ssm], axis=1).astype(BF16)
        h = h + jnp.dot(cat, wout_ref[...], preferred_element_type=F32)

        out = _ffn_block(h, gffn_ref, wg_ref, wu_ref, wd_ref)
        if final:
            obuf_ref = reorder_ref[-1]
            out = _rmsnorm(out, gfin_ref[...])
            t0 = sb * TIME_BLOCK
            for k in range(LANE_SLABS):
                obuf_ref[k] = out[:, k * V7X_LANES:(k + 1) * V7X_LANES]
                for b in range(BATCH):
                    o_ref[b, t0:t0 + TIME_BLOCK, k * V7X_LANES:(k + 1) * V7X_LANES] = (
                        obuf_ref[k, pl.ds(b, TIME_BLOCK, stride=BATCH), :])
        else:
            o_ref[rows, :] = out
    hist_ref[...] = hist
    for j in range(SSM_HALVES):
        state_ref[j] = v[j]

    for src, dst in zip(next_f32, next_bf16):
        dst[...] = src[...].astype(BF16)


def _layer_spec(a, layer):
    tail = (0,) * (a.ndim - 1)
    return pl.BlockSpec((None,) + a.shape[1:], lambda i: (layer,) + tail,
                        pipeline_mode=pl.Buffered(1))


def _resident_spec(a):
    zeros = (0,) * a.ndim
    return pl.BlockSpec(a.shape, lambda i: zeros, pipeline_mode=pl.Buffered(1))


def _cast_specs(w, layer):
    rows, cols = w.shape[1:]
    block_rows = CAST_BLOCK_ROWS if rows % N_STEPS or rows // N_STEPS % 16 else rows // N_STEPS
    last = rows // block_rows - 1
    assert rows % block_rows == 0 and last < N_STEPS
    src = pl.BlockSpec((None, block_rows, cols), lambda i: (layer, jnp.minimum(i, last), 0))
    dst = pl.BlockSpec((block_rows, cols), lambda i: (jnp.minimum(i, last), 0))
    return src, dst, jax.ShapeDtypeStruct((rows, cols), BF16)


def _layer(h, layer, small, weights, gfin, weights_f32):
    first = layer == 0
    final = layer == DEPTH - 1
    row_spec = pl.BlockSpec((STEP_ROWS, D_MODEL), lambda i: (i, 0))
    btd_spec = pl.BlockSpec((BATCH, STEP_TIME, D_MODEL), lambda i: (0, i, 0))
    scratch = [
        pltpu.VMEM((POOL_HIST_ROWS, D_POOL), F32),
        pltpu.VMEM((SSM_HALVES, V7X_SUBLANES, STATE_PER_HALF), F32),
        pltpu.VMEM((SUB_BLOCKS, BLOCK_ROWS + POOL_HIST_ROWS, D_POOL), F32),
        pltpu.VMEM((SUB_BLOCKS, SSM_HALVES, BLOCK_ROWS, RE_IM * STATE_PER_HALF), F32),
    ]
    if first:
        scratch.append(pltpu.VMEM((LANE_SLABS, BLOCK_ROWS, V7X_LANES), F32))
    if final:
        scratch.append(pltpu.VMEM((LANE_SLABS, BLOCK_ROWS, V7X_LANES), F32))
    in_specs = ([btd_spec if first else row_spec] + [_layer_spec(c, layer) for c in small]
                + [_resident_spec(w) for w in weights] + [_resident_spec(gfin)])
    operands = [h, *small, *weights, gfin]
    if final:
        out_specs = [btd_spec]
        out_shape = [jax.ShapeDtypeStruct((BATCH, SEQ, D_MODEL), F32)]
    else:
        out_specs = [row_spec]
        out_shape = [jax.ShapeDtypeStruct((SEQ * BATCH, D_MODEL), F32)]
        for w in weights_f32:
            src, dst, shape = _cast_specs(w, layer + 1)
            in_specs.append(src)
            operands.append(w)
            out_specs.append(dst)
            out_shape.append(shape)
    outs = pl.pallas_call(
        functools.partial(_layer_kernel, first=first, final=final),
        grid=(N_STEPS,),
        in_specs=in_specs,
        out_specs=out_specs,
        out_shape=out_shape,
        scratch_shapes=scratch,
        compiler_params=pltpu.CompilerParams(
            dimension_semantics=("arbitrary",), vmem_limit_bytes=V7X_VMEM_LIMIT_BYTES),
        name="layer_%d" % layer,
    )(*operands)
    return outs[0], outs[1:]


def kernel(x, norm_mix, w_in, w_pool, pool_scale, lam_re, lam_im, log_dt, b_re, b_im, c_re, c_im,
           d_skip, w_glu, b_glu, w_out, norm_ffn, w_gate, w_up, w_down, norm_final):
    assert x.shape == (BATCH, SEQ, D_MODEL) and x.dtype == F32

    weights_f32 = (w_in, w_pool.reshape(DEPTH, D_POOL, POOL_GROUP), w_glu, w_out, w_gate, w_up,
                   w_down)
    (abar_r, abar_i, wb, wc), weights = _ssm_prep(lam_re, lam_im, log_dt, b_re, b_im, c_re, c_im,
                                                  weights_f32)
    per_row = (DEPTH, SSM_HALVES, GROUPS_PER_HALF, SSM_GROUP, SSM_STATE)
    abar_r = abar_r.reshape(per_row)[:, :, :, 0, :].reshape(DEPTH, SSM_HALVES, 1, STATE_PER_HALF)
    abar_i = abar_i.reshape(per_row)[:, :, :, 0, :].reshape(DEPTH, SSM_HALVES, 1, STATE_PER_HALF)
    a_r = jnp.broadcast_to(abar_r, (DEPTH, SSM_HALVES, V7X_SUBLANES, STATE_PER_HALF))
    sign = jnp.where(jnp.arange(V7X_SUBLANES) < BATCH, -1.0, 1.0).astype(F32)[None, None, :, None]
    a_i = sign * abar_i
    wb = wb.reshape(DEPTH, SSM_HALVES, CH_PER_HALF, RE_IM * STATE_PER_HALF)
    wc = wc.reshape(DEPTH, SSM_HALVES, RE_IM * STATE_PER_HALF, CH_PER_HALF)

    small = (
        norm_mix.reshape(DEPTH, 1, D_MODEL), pool_scale.reshape(DEPTH, 1, D_POOL), wb, a_r, a_i, wc,
        d_skip.reshape(DEPTH, 1, D_SSM), b_glu.reshape(DEPTH, 1, D_SSM),
        norm_ffn.reshape(DEPTH, 1, D_MODEL))
    assert len(small) == N_SMALL_PARAMS
    gfin = norm_final.reshape(1, D_MODEL)

    h = x
    for layer in range(DEPTH):
        h, weights = _layer(h, layer, small, weights, gfin, weights_f32)
    return h
```
